```python
import jax
import jax.numpy as jnp
from jax import lax
import numpy as np

D_MODEL = 1024
BATCH = 8
SEQ = 2048
DEPTH = 2

D_MIX = 2 * D_MODEL
EPS = 1e-6
NEG_INF = -1e30

ATT_WIDTH = D_MIX // 4
ATT_HEAD_DIM = 64
ATT_HEADS = ATT_WIDTH // ATT_HEAD_DIM
MOBA_BLOCK = 256
MOBA_TOPK = 3
MOBA_Q_CHUNK = 32

SSM_WIDTH = D_MIX // 2
SSM_HEAD_DIM = 64
SSM_HEADS = SSM_WIDTH // SSM_HEAD_DIM
SSM_GROUPS = 2
SSM_STATE = 128
SSM_CONV = 4
SSM_CHUNK = 128
SSM_CONV_DIM = SSM_WIDTH + 2 * SSM_GROUPS * SSM_STATE

MLSTM_WIDTH = D_MIX // 4
MLSTM_HEAD_DIM = 128
MLSTM_HEADS = MLSTM_WIDTH // MLSTM_HEAD_DIM
MLSTM_CONV = 4
MLSTM_CHUNK = 128

FFN_HIDDEN = 11 * D_MODEL // 4
FFN_CONV = 3

PROJ_SIZES = (ATT_WIDTH, ATT_WIDTH, ATT_WIDTH,
              SSM_WIDTH, SSM_CONV_DIM, SSM_HEADS,
              2 * MLSTM_WIDTH, MLSTM_WIDTH, MLSTM_WIDTH,
              MLSTM_HEADS, MLSTM_HEADS)
P_IN = sum(PROJ_SIZES)

kernel_name = "hybrid_moba_ssd_mlstm_convglu"


def rmsnorm(x, w):
    xf = x.astype(jnp.float32)
    y = xf * lax.rsqrt(jnp.mean(xf * xf, axis=-1, keepdims=True) + EPS)
    return (y * w.astype(jnp.float32)).astype(x.dtype)


def causal_dwconv(x, w, b):
    k_width, chans = w.shape
    y = lax.conv_general_dilated(x, w[:, None, :].astype(x.dtype), window_strides=(1,),
                                 padding=[(k_width - 1, 0)],
                                 dimension_numbers=('NWC', 'WIO', 'NWC'),
                                 feature_group_count=chans)
    return y + b.astype(x.dtype)


def split_columns(t, sizes):
    offs = np.cumsum(np.array(sizes))[:-1]
    return jnp.split(t, [int(o) for o in offs], axis=-1)


def moba_attention(q, k, v):
    bsz, seq, heads, dh = q.shape
    n_blocks = -(-seq // MOBA_BLOCK)
    n_gate = max(n_blocks, MOBA_TOPK)
    pad = n_blocks * MOBA_BLOCK - seq
    q = q.transpose(0, 2, 1, 3)
    k = jnp.pad(k.transpose(0, 2, 1, 3), ((0, 0), (0, 0), (0, pad), (0, 0)))
    v = jnp.pad(v.transpose(0, 2, 1, 3), ((0, 0), (0, 0), (0, pad), (0, 0)))
    k_blocks = k.reshape(bsz, heads, n_blocks, MOBA_BLOCK, dh)
    v_blocks = v.reshape(bsz, heads, n_blocks, MOBA_BLOCK, dh)
    k_mean = jnp.mean(k_blocks.astype(jnp.float32), axis=3)
    k_mean = jnp.pad(k_mean, ((0, 0), (0, 0), (0, n_gate - n_blocks), (0, 0)))
    n_chunks = seq // MOBA_Q_CHUNK
    q_chunks = jnp.moveaxis(q.reshape(bsz, heads, n_chunks, MOBA_Q_CHUNK, dh), 2, 0)
    b_idx = jnp.arange(bsz)[:, None, None, None]
    h_idx = jnp.arange(heads)[None, :, None, None]
    slot = jnp.arange(MOBA_TOPK + 1)
    is_routed = slot < MOBA_TOPK
    key_off = jnp.arange(MOBA_BLOCK)
    block_ids = jnp.arange(n_gate)
    scale = dh ** -0.5

    def attend_chunk(args):
        qc, c = args
        start = c * MOBA_Q_CHUNK
        own = start // MOBA_BLOCK
        q_pos = start + jnp.arange(MOBA_Q_CHUNK)
        gate = jnp.einsum('bhqd,bhnd->bhqn', qc.astype(jnp.float32), k_mean)
        gate = jnp.where(block_ids < own, gate, NEG_INF)
        _, top_idx = lax.top_k(gate, MOBA_TOPK)
        top_idx = jnp.minimum(top_idx, n_blocks - 1)
        own_idx = jnp.full(top_idx.shape[:-1] + (1,), own, dtype=top_idx.dtype)
        sel = jnp.concatenate([top_idx, own_idx], axis=-1)
        k_sel = k_blocks[b_idx, h_idx, sel]
        v_sel = v_blocks[b_idx, h_idx, sel]
        s = jnp.einsum('bhqd,bhqrkd->bhqrk', qc, k_sel,
                       preferred_element_type=jnp.float32) * scale
        causal = (own * MOBA_BLOCK + key_off)[None, :] <= q_pos[:, None]
        valid = jnp.where(is_routed[None, :, None], (slot < own)[None, :, None],
                          causal[:, None, :])
        p = jax.nn.softmax(jnp.where(valid, s, NEG_INF), axis=(-2, -1))
        return jnp.einsum('bhqrk,bhqrkd->bhqd', p.astype(v_sel.dtype), v_sel)

    out = lax.map(attend_chunk, (q_chunks, jnp.arange(n_chunks)))
    return out.transpose(1, 0, 3, 2, 4).reshape(bsz, seq, heads * dh)


def mamba2_ssd(z, xbc, dt_raw, conv_w, conv_b, dt_bias, a_log, d_skip, norm_w):
    bsz, seq, _ = z.shape
    hpg = SSM_HEADS // SSM_GROUPS
    n_chunks = seq // SSM_CHUNK
    xbc = jax.nn.silu(causal_dwconv(xbc, conv_w, conv_b)).astype(jnp.float32)
    xs, b_in, c_in = jnp.split(xbc, [SSM_WIDTH, SSM_WIDTH + SSM_GROUPS * SSM_STATE], axis=-1)
    xs = xs.reshape(bsz, n_chunks, SSM_CHUNK, SSM_GROUPS, hpg, SSM_HEAD_DIM)
    b_in = b_in.reshape(bsz, n_chunks, SSM_CHUNK, SSM_GROUPS, SSM_STATE)
    c_in = c_in.reshape(bsz, n_chunks, SSM_CHUNK, SSM_GROUPS, SSM_STATE)
    dt = jax.nn.softplus(dt_raw.astype(jnp.float32) + dt_bias.astype(jnp.float32))
    dt = dt.reshape(bsz, n_chunks, SSM_CHUNK, SSM_GROUPS, hpg)
    a = -jnp.exp(a_log.astype(jnp.float32)).reshape(SSM_GROUPS, hpg)
    log_decay = jnp.cumsum(dt * a, axis=2)
    x_dt = xs * dt[..., None]
    causal = jnp.tril(jnp.ones((SSM_CHUNK, SSM_CHUNK), dtype=bool))[:, :, None, None]
    seg = log_decay[:, :, :, None] - log_decay[:, :, None, :]
    decay = jnp.exp(jnp.where(causal, seg, -jnp.inf))
    cb = jnp.einsum('bclgn,bcsgn->bclsg', c_in, b_in)
    y = jnp.einsum('bclsg,bclsge,bcsgep->bclgep', cb, decay, x_dt)
    to_end = jnp.exp(log_decay[:, :, -1:] - log_decay)
    chunk_states = jnp.einsum('bclgn,bclge,bclgep->bcgepn', b_in, to_end, x_dt)
    chunk_decay = jnp.exp(log_decay[:, :, -1])

    def carry_state(state, inp):
        s_chunk, d_chunk = inp
        return state * d_chunk[..., None, None] + s_chunk, state

    init = jnp.zeros((bsz, SSM_GROUPS, hpg, SSM_HEAD_DIM, SSM_STATE), jnp.float32)
    _, start = lax.scan(carry_state, init,
                        (jnp.moveaxis(chunk_states, 1, 0), jnp.moveaxis(chunk_decay, 1, 0)))
    start = jnp.moveaxis(start, 0, 1)
    y = y + jnp.einsum('bclgn,bcgepn,bclge->bclgep', c_in, start, jnp.exp(log_decay))
    y = y + xs * d_skip.astype(jnp.float32).reshape(SSM_GROUPS, hpg)[:, :, None]
    y = y.reshape(bsz, seq, SSM_WIDTH) * jax.nn.silu(z.astype(jnp.float32))
    y = rmsnorm(y.reshape(bsz, seq, SSM_GROUPS, SSM_WIDTH // SSM_GROUPS),
                norm_w.reshape(SSM_GROUPS, SSM_WIDTH // SSM_GROUPS))
    return y.reshape(bsz, seq, SSM_WIDTH).astype(z.dtype)


def mlstm_chunkwise(qk_pre, v, o_pre, i_pre, f_pre, conv_w, conv_b, i_bias, f_bias, norm_w):
    out_dtype = o_pre.dtype
    bsz, seq, _ = v.shape
    n_chunks = seq // MLSTM_CHUNK
    shp = (bsz, n_chunks, MLSTM_CHUNK, MLSTM_HEADS, MLSTM_HEAD_DIM)
    gshp = shp[:-1]
    qk = jax.nn.silu(causal_dwconv(qk_pre, conv_w, conv_b)).astype(jnp.float32)
    q, k = jnp.split(qk, 2, axis=-1)
    q = q.reshape(shp)
    k = k.reshape(shp) * (MLSTM_HEAD_DIM ** -0.5)
    v = v.astype(jnp.float32).reshape(shp)
    log_i = (i_pre.astype(jnp.float32) + i_bias.astype(jnp.float32)).reshape(gshp)
    log_f = jax.nn.log_sigmoid(f_pre.astype(jnp.float32) + f_bias.astype(jnp.float32)).reshape(gshp)
    cum_f = jnp.cumsum(log_f, axis=2)
    tot_f = cum_f[:, :, -1]
    a = tot_f[:, :, None] - cum_f + log_i
    m_loc = jnp.max(a, axis=2)
    w_loc = jnp.exp(a - m_loc[:, :, None])
    c_loc = jnp.einsum('bclh,bclhd,bclhe->bchde', w_loc, v, k)
    n_loc = jnp.einsum('bclh,bclhe->bche', w_loc, k)

    def carry_state(carry, inp):
        c_st, n_st, m_st = carry
        cl, nl, ml, gc = inp
        m_new = jnp.maximum(gc + m_st, ml)
        s_old = jnp.exp(gc + m_st - m_new)
        s_new = jnp.exp(ml - m_new)
        new = (c_st * s_old[..., None, None] + cl * s_new[..., None, None],
               n_st * s_old[..., None] + nl * s_new[..., None],
               m_new)
        return new, carry

    init = (jnp.zeros((bsz, MLSTM_HEADS, MLSTM_HEAD_DIM, MLSTM_HEAD_DIM), jnp.float32),
            jnp.zeros((bsz, MLSTM_HEADS, MLSTM_HEAD_DIM), jnp.float32),
            jnp.zeros((bsz, MLSTM_HEADS), jnp.float32))
    _, (c_start, n_start, m_start) = lax.scan(
        carry_state, init,
        (jnp.moveaxis(c_loc, 1, 0), jnp.moveaxis(n_loc, 1, 0),
         jnp.moveaxis(m_loc, 1, 0), jnp.moveaxis(tot_f, 1, 0)))
    c_start = jnp.moveaxis(c_start, 0, 1)
    n_start = jnp.moveaxis(n_start, 0, 1)
    m_start = jnp.moveaxis(m_start, 0, 1)
    causal = jnp.tril(jnp.ones((MLSTM_CHUNK, MLSTM_CHUNK), dtype=bool))[:, :, None]
    d = cum_f[:, :, :, None] - cum_f[:, :, None, :] + log_i[:, :, None, :]
    d = jnp.where(causal, d, -jnp.inf)
    inter = cum_f + m_start[:, :, None]
    m_row = jnp.maximum(jnp.max(d, axis=3), inter)
    w_intra = jnp.exp(d - m_row[:, :, :, None])
    w_inter = jnp.exp(inter - m_row)
    scores = jnp.einsum('bcthd,bcshd->bctsh', q, k) * w_intra
    num = (jnp.einsum('bctsh,bcshd->bcthd', scores, v)
           + w_inter[..., None] * jnp.einsum('bchde,bcthe->bcthd', c_start, q))
    den = jnp.sum(scores, axis=3) + w_inter * jnp.einsum('bche,bcthe->bcth', n_start, q)
    h = num / jnp.maximum(jnp.abs(den), jnp.exp(-m_row))[..., None]
    h = rmsnorm(h, norm_w.reshape(MLSTM_HEADS, MLSTM_HEAD_DIM))
    h = jax.nn.sigmoid(o_pre.astype(jnp.float32)).reshape(shp) * h
    return h.reshape(bsz, seq, MLSTM_WIDTH).astype(out_dtype)


def conv_glu(h, w_up, conv_w, conv_b, w_down):
    gate, val = jnp.split(jnp.einsum('bsd,df->bsf', h, w_up), 2, axis=-1)
    gate = causal_dwconv(gate, conv_w, conv_b)
    return jnp.einsum('bsf,fd->bsd', jax.nn.gelu(gate, approximate=False) * val, w_down)


def setup_inputs(seed: int = 0) -> dict:
    key = jax.random.key(seed)
    ks = jax.random.split(key, 24)
    f32 = jnp.float32

    def nrm(k, shape, scale):
        return jax.random.normal(k, shape, f32) * scale

    def gain(k, shape):
        return 1.0 + 0.02 * jax.random.normal(k, shape, f32)

    dt0 = jnp.exp(jax.random.uniform(ks[7], (DEPTH, SSM_HEADS), f32,
                                     minval=float(np.log(1e-3)), maxval=float(np.log(1e-1))))
    return {
        "x": jax.random.normal(ks[0], (BATCH, SEQ, D_MODEL), f32),
        "norm1_w": gain(ks[1], (DEPTH, D_MODEL)),
        "w_in": nrm(ks[2], (DEPTH, D_MODEL, P_IN), D_MODEL ** -0.5),
        "q_norm_w": gain(ks[3], (DEPTH, ATT_HEAD_DIM)),
        "k_norm_w": gain(ks[4], (DEPTH, ATT_HEAD_DIM)),
        "ssm_conv_w": nrm(ks[5], (DEPTH, SSM_CONV, SSM_CONV_DIM), SSM_CONV ** -0.5),
        "ssm_conv_b": nrm(ks[6], (DEPTH, SSM_CONV_DIM), 0.02),
        "ssm_dt_bias": dt0 + jnp.log(-jnp.expm1(-dt0)),
        "ssm_a_log": jnp.log(jax.random.uniform(ks[8], (DEPTH, SSM_HEADS), f32, minval=1.0, maxval=16.0)),
        "ssm_d": gain(ks[9], (DEPTH, SSM_HEADS)),
        "ssm_norm_w": gain(ks[10], (DEPTH, SSM_WIDTH)),
        "mlstm_conv_w": nrm(ks[11], (DEPTH, MLSTM_CONV, 2 * MLSTM_WIDTH), MLSTM_CONV ** -0.5),
        "mlstm_conv_b": nrm(ks[12], (DEPTH, 2 * MLSTM_WIDTH), 0.02),
        "mlstm_i_bias": nrm(ks[13], (DEPTH, MLSTM_HEADS), 0.1),
        "mlstm_f_bias": jnp.linspace(3.0, 6.0, MLSTM_HEADS, dtype=f32)[None, :]
                        + nrm(ks[14], (DEPTH, MLSTM_HEADS), 0.02),
        "mlstm_norm_w": gain(ks[15], (DEPTH, MLSTM_WIDTH)),
        "w_out": nrm(ks[16], (DEPTH, D_MIX, D_MODEL), D_MIX ** -0.5),
        "norm2_w": gain(ks[17], (DEPTH, D_MODEL)),
        "ffn_w_up": nrm(ks[18], (DEPTH, D_MODEL, 2 * FFN_HIDDEN), D_MODEL ** -0.5),
        "ffn_conv_w": nrm(ks[19], (DEPTH, FFN_CONV, FFN_HIDDEN), FFN_CONV ** -0.5),
        "ffn_conv_b": nrm(ks[20], (DEPTH, FFN_HIDDEN), 0.02),
        "ffn_w_down": nrm(ks[21], (DEPTH, FFN_HIDDEN, D_MODEL), FFN_HIDDEN ** -0.5),
    }


def reference(x, norm1_w, w_in, q_norm_w, k_norm_w, ssm_conv_w, ssm_conv_b, ssm_dt_bias,
              ssm_a_log, ssm_d, ssm_norm_w, mlstm_conv_w, mlstm_conv_b, mlstm_i_bias,
              mlstm_f_bias, mlstm_norm_w, w_out, norm2_w, ffn_w_up, ffn_conv_w, ffn_conv_b,
              ffn_w_down):
    bsz, seq, _ = x.shape
    for l in range(DEPTH):
        h = rmsnorm(x, norm1_w[l])
        proj = jnp.einsum('bsd,dp->bsp', h, w_in[l])
        (aq, ak, av, z, xbc, dt_raw, m_qk, m_v, m_o, m_i, m_f) = split_columns(proj, PROJ_SIZES)
        head_shape = (bsz, seq, ATT_HEADS, ATT_HEAD_DIM)
        aq = rmsnorm(aq.reshape(head_shape), q_norm_w[l])
        ak = rmsnorm(ak.reshape(head_shape), k_norm_w[l])
        y_att = moba_attention(aq, ak, av.reshape(head_shape))
        y_ssm = mamba2_ssd(z, xbc, dt_raw, ssm_conv_w[l], ssm_conv_b[l], ssm_dt_bias[l],
                           ssm_a_log[l], ssm_d[l], ssm_norm_w[l])
        y_mlstm = mlstm_chunkwise(m_qk, m_v, m_o, m_i, m_f, mlstm_conv_w[l], mlstm_conv_b[l],
                                  mlstm_i_bias[l], mlstm_f_bias[l], mlstm_norm_w[l])
        mix = jnp.concatenate([y_att, y_ssm, y_mlstm], axis=-1)
        x = x + jnp.einsum('bsm,md->bsd', mix, w_out[l])
        h = rmsnorm(x, norm2_w[l])
        x = x + conv_glu(h, ffn_w_up[l], ffn_conv_w[l], ffn_conv_b[l], ffn_w_down[l])
    return x
```

```python
import functools

import numpy as np
import jax
import jax.numpy as jnp
from jax import lax
from jax.experimental import pallas as pl
from jax.experimental.pallas import tpu as pltpu

F32 = jnp.float32
BF16 = jnp.bfloat16

D_MODEL = 1024
SEQ = 2048
D_MIX = 2 * D_MODEL
EPS = 1e-6
NEG_INF = -1e30

ATT_WIDTH = D_MIX // 4
ATT_HEAD_DIM = 64
ATT_HEADS = ATT_WIDTH // ATT_HEAD_DIM
MOBA_BLOCK = 256
MOBA_TOPK = 3
N_MOBA_BLOCKS = SEQ // MOBA_BLOCK

SSM_WIDTH = D_MIX // 2
SSM_HEAD_DIM = 64
SSM_HEADS = SSM_WIDTH // SSM_HEAD_DIM
SSM_GROUPS = 2
SSM_STATE = 128
SSM_CONV = 4
SSM_CHUNK = 128
SSM_CONV_DIM = SSM_WIDTH + 2 * SSM_GROUPS * SSM_STATE
SSM_GROUP_WIDTH = SSM_WIDTH // SSM_GROUPS

MLSTM_WIDTH = D_MIX // 4
MLSTM_HEAD_DIM = 128
MLSTM_HEADS = MLSTM_WIDTH // MLSTM_HEAD_DIM
MLSTM_CONV = 4
MLSTM_CHUNK = 128

FFN_HIDDEN = 11 * D_MODEL // 4
FFN_CONV = 3

PROJ_SIZES = (ATT_WIDTH, ATT_WIDTH, ATT_WIDTH,
              SSM_WIDTH, SSM_CONV_DIM, SSM_HEADS,
              2 * MLSTM_WIDTH, MLSTM_WIDTH, MLSTM_WIDTH,
              MLSTM_HEADS, MLSTM_HEADS)

LANES = 128
SUBLANES = 8
CHUNK = 128
SMALL_DT = 0
SMALL_I = SSM_HEADS
SMALL_F = SSM_HEADS + MLSTM_HEADS
VMEM_LIMIT = 56 * 1024 * 1024


def _dot(a, b):
    return jnp.dot(a, b, preferred_element_type=F32)


def _dot_nt(a, b):
    return lax.dot_general(a, b, (((1,), (1,)), ((), ())), preferred_element_type=F32)


def _dot_tn(a, b):
    return lax.dot_general(a, b, (((0,), (0,)), ((), ())), preferred_element_type=F32)


def _split3(v):
    hi = v.astype(BF16)
    r1 = v - hi.astype(F32)
    mid = r1.astype(BF16)
    lo = (r1 - mid.astype(F32)).astype(BF16)
    return hi, mid, lo


def _dot_exact_lhs(e, v):
    hi, mid, lo = _split3(v)
    return _dot(e, hi) + _dot(e, mid) + _dot(e, lo)


def _dot_exact_rhs(v, e):
    hi, mid, lo = _split3(v)
    return _dot(hi, e) + _dot(mid, e) + _dot(lo, e)


def _sigmoid(x):
    return 1.0 / (1.0 + jnp.exp(-x))


def _silu(x):
    return x * _sigmoid(x)


def _softplus(x):
    return jnp.maximum(x, 0.0) + jnp.log1p(jnp.exp(-jnp.abs(x)))


def _tril_mask(n):
    r = lax.broadcasted_iota(jnp.int32, (n, n), 0)
    c = lax.broadcasted_iota(jnp.int32, (n, n), 1)
    return c <= r


IN_TM = 256
IN_GROUPS = (3 * ATT_WIDTH, SSM_WIDTH, SSM_CONV_DIM, 2 * MLSTM_WIDTH, MLSTM_WIDTH, MLSTM_WIDTH, LANES)
IN_COLS = sum(IN_GROUPS)
DOT_N = 512


def _inproj_kernel(x_ref, nw_ref, w_ref, *out_refs):
    x = x_ref[...]
    ms = jnp.mean(x * x, axis=-1, keepdims=True)
    h = (x * lax.rsqrt(ms + EPS) * nw_ref[...]).astype(BF16)
    off = 0
    for ref in out_refs:
        n = ref.shape[-1]
        for c in range(0, n, DOT_N):
            cw = min(DOT_N, n - c)
            ref[:, c:c + cw] = _dot(h, w_ref[:, off + c:off + c + cw]).astype(ref.dtype)
        off += n


def _in_projection(x2d, norm_w, w_cat):
    t = x2d.shape[0]
    const = lambda i: (0, 0)
    row = lambda i: (i, 0)
    return pl.pallas_call(
        _inproj_kernel,
        grid=(t // IN_TM,),
        in_specs=[pl.BlockSpec((IN_TM, D_MODEL), row),
                  pl.BlockSpec((1, D_MODEL), const),
                  pl.BlockSpec((D_MODEL, IN_COLS), const)],
        out_specs=[pl.BlockSpec((IN_TM, n), row) for n in IN_GROUPS],
        out_shape=[jax.ShapeDtypeStruct((t, n), F32) for n in IN_GROUPS],
        compiler_params=pltpu.CompilerParams(dimension_semantics=("arbitrary",),
                                             vmem_limit_bytes=VMEM_LIMIT),
        name="in_projection",
    )(x2d, norm_w, w_cat)


ATT_TQ = 128
HEADS_PER_STEP = LANES // ATT_HEAD_DIM


def _moba_kernel(q_ref, k_ref, v_ref, qw_ref, kw_ref, o_ref, qa_ref, ka_ref, vb_ref):
    nb = N_MOBA_BLOCKS
    row_blk = lax.broadcasted_iota(jnp.int32, (SEQ, nb), 0) // MOBA_BLOCK
    col_blk = lax.broadcasted_iota(jnp.int32, (SEQ, nb), 1)
    pool = jnp.where(lax.broadcasted_iota(jnp.int32, (nb, SEQ), 1) // MOBA_BLOCK
                     == lax.broadcasted_iota(jnp.int32, (nb, SEQ), 0),
                     1.0 / MOBA_BLOCK, 0.0).astype(BF16)
    pad = jnp.zeros((SEQ, LANES - ATT_HEAD_DIM - nb), F32)

    for hh in range(HEADS_PER_STEP):
        lo = hh * ATT_HEAD_DIM
        q = q_ref[0, :, lo:lo + ATT_HEAD_DIM]
        k = k_ref[0, :, lo:lo + ATT_HEAD_DIM]
        qn = q * lax.rsqrt(jnp.mean(q * q, axis=-1, keepdims=True) + EPS) * qw_ref[...]
        kn = k * lax.rsqrt(jnp.mean(k * k, axis=-1, keepdims=True) + EPS) * kw_ref[...]
        k_mean = _dot_exact_lhs(pool, kn)
        gate = lax.dot_general(qn, k_mean, (((1,), (1,)), ((), ())),
                               precision=lax.Precision.HIGHEST,
                               preferred_element_type=F32)
        valid = col_blk < row_blk
        gm = jnp.where(valid, gate, NEG_INF)
        rank = jnp.zeros((SEQ, nb), jnp.int32)
        for j in range(nb):
            gj = gm[:, j:j + 1]
            ahead = (gj > gm) | ((gj == gm) & (j < col_blk))
            rank = rank + ahead.astype(jnp.int32)
        allowed = (valid & (rank < MOBA_TOPK)) | (col_blk == row_blk)
        bias = jnp.where(allowed, 0.0, NEG_INF)
        onehot = jnp.where(col_blk == row_blk, 1.0, 0.0)
        scale = ATT_HEAD_DIM ** -0.5
        qa_ref[...] = jnp.concatenate([qn * scale, bias, pad], axis=1).astype(BF16)
        ka_ref[...] = jnp.concatenate([kn, onehot, pad], axis=1).astype(BF16)
        vb_ref[...] = v_ref[0, :, lo:lo + ATT_HEAD_DIM].astype(BF16)

        def q_tile(qt, carry):
            r0 = pl.multiple_of(qt * ATT_TQ, ATT_TQ)
            own = qt // (MOBA_BLOCK // ATT_TQ)
            qa = qa_ref[pl.ds(r0, ATT_TQ), :]
            d0 = pl.multiple_of(own * MOBA_BLOCK, MOBA_BLOCK)
            s = _dot_nt(qa, ka_ref[pl.ds(d0, MOBA_BLOCK), :])
            q_pos = r0 + lax.broadcasted_iota(jnp.int32, (ATT_TQ, MOBA_BLOCK), 0)
            k_pos = d0 + lax.broadcasted_iota(jnp.int32, (ATT_TQ, MOBA_BLOCK), 1)
            s = jnp.where(k_pos <= q_pos, s, NEG_INF)
            m = jnp.max(s, axis=-1, keepdims=True)
            p = jnp.exp(s - m)
            l = jnp.sum(p, axis=-1, keepdims=True)
            acc = _dot(p.astype(BF16), vb_ref[pl.ds(d0, MOBA_BLOCK), :])

            def kv_block(kj, c):
                m, l, acc = c
                k0 = pl.multiple_of(kj * MOBA_BLOCK, MOBA_BLOCK)
                s = _dot_nt(qa, ka_ref[pl.ds(k0, MOBA_BLOCK), :])
                m_new = jnp.maximum(m, jnp.max(s, axis=-1, keepdims=True))
                alpha = jnp.exp(m - m_new)
                p = jnp.exp(s - m_new)
                l = alpha * l + jnp.sum(p, axis=-1, keepdims=True)
                acc = alpha * acc + _dot(p.astype(BF16), vb_ref[pl.ds(k0, MOBA_BLOCK), :])
                return m_new, l, acc

            m, l, acc = lax.fori_loop(0, own, kv_block, (m, l, acc))
            o_ref[0, pl.ds(r0, ATT_TQ), lo:lo + ATT_HEAD_DIM] = acc / l
            return carry

        lax.fori_loop(0, SEQ // ATT_TQ, q_tile, 0)


def _moba_attention(qkv, q_norm_w, k_norm_w, bsz):
    qkv3 = qkv.reshape(bsz, SEQ, 3 * ATT_WIDTH)
    n_pairs = ATT_WIDTH // LANES
    blk = lambda off: pl.BlockSpec((1, SEQ, LANES), lambda b, h: (b, 0, off + h))
    const = lambda b, h: (0, 0)
    out = pl.pallas_call(
        _moba_kernel,
        grid=(bsz, n_pairs),
        in_specs=[blk(0), blk(n_pairs), blk(2 * n_pairs),
                  pl.BlockSpec((1, ATT_HEAD_DIM), const),
                  pl.BlockSpec((1, ATT_HEAD_DIM), const)],
        out_specs=pl.BlockSpec((1, SEQ, LANES), lambda b, h: (b, 0, h)),
        out_shape=jax.ShapeDtypeStruct((bsz, SEQ, ATT_WIDTH), F32),
        scratch_shapes=[pltpu.VMEM((SEQ, LANES), BF16),
                        pltpu.VMEM((SEQ, LANES), BF16),
                        pltpu.VMEM((SEQ, ATT_HEAD_DIM), BF16)],
        compiler_params=pltpu.CompilerParams(dimension_semantics=("arbitrary", "arbitrary"),
                                             vmem_limit_bytes=VMEM_LIMIT),
        name="moba_attention",
    )(qkv3, qkv3, qkv3, q_norm_w, k_norm_w)
    return out.reshape(bsz * SEQ, ATT_WIDTH)


def _chunk_conv(x, w_ref, b_ref, pad_ref, first, width, rows):
    @pl.when(first)
    def _():
        pad_ref[0:SUBLANES, :] = jnp.zeros((SUBLANES, pad_ref.shape[1]), F32)

    pad_ref[SUBLANES:SUBLANES + rows, :] = x
    acc = b_ref[...] + w_ref[width - 1:width, :] * x
    for j in range(width - 1):
        s = SUBLANES - (width - 1) + j
        acc = acc + w_ref[j:j + 1, :] * pad_ref[s:s + rows, :]
    pad_ref[0:SUBLANES, :] = pad_ref[rows:rows + SUBLANES, :]
    return acc


def _ssd_kernel(z_ref, xbc_ref, small_ref, cw_ref, cb_ref, dtb_ref, alog_ref, dx_ref, nw_ref,
                e64_ref, e128_ref, o_ref, pad_ref, st_ref, y_ref):
    first = pl.program_id(1) == 0

    @pl.when(first)
    def _():
        st_ref[...] = jnp.zeros(st_ref.shape, F32)

    xbc = _silu(_chunk_conv(xbc_ref[0], cw_ref, cb_ref, pad_ref, first, SSM_CONV, CHUNK))
    xs = xbc[:, :SSM_WIDTH]
    gn = SSM_GROUPS * SSM_STATE
    b_in = xbc[:, SSM_WIDTH:SSM_WIDTH + gn].astype(BF16)
    c_in = xbc[:, SSM_WIDTH + gn:].astype(BF16)

    lane = lax.broadcasted_iota(jnp.int32, (1, LANES), 1)
    head_lane = lane < SSM_HEADS
    dt = _softplus(small_ref[0] + dtb_ref[...])
    a = jnp.where(head_lane, -jnp.exp(alog_ref[...]), 0.0)
    tril = _tril_mask(CHUNK)
    tril_b = jnp.where(tril, 1.0, 0.0).astype(BF16)
    ld = _dot_exact_lhs(tril_b, dt * a)
    ld_t = ld.T
    dt_t = dt.T
    eld = jnp.exp(ld)
    w_state = dt * jnp.exp(ld[CHUNK - 1:CHUNK, :] - ld)
    e64 = e64_ref[...]
    eld_x = _dot_exact_rhs(eld, e64)
    w_x = _dot_exact_rhs(w_state, e64)
    ld_b = _dot_exact_rhs(ld, e128_ref[...])

    lane2 = lax.broadcasted_iota(jnp.int32, (CHUNK, LANES), 1)
    hpg = SSM_HEADS // SSM_GROUPS
    for g in range(SSM_GROUPS):
        cb = _dot_nt(c_in[:, g * SSM_STATE:(g + 1) * SSM_STATE],
                     b_in[:, g * SSM_STATE:(g + 1) * SSM_STATE])
        for pair in range(hpg // 2):
            e0 = g * hpg + 2 * pair
            ms = []
            for e in (e0, e0 + 1):
                seg = ld_b[:, e * LANES:(e + 1) * LANES] - ld_t[e:e + 1, :]
                dec = jnp.exp(jnp.where(tril, seg, NEG_INF))
                ms.append(cb * dec * dt_t[e:e + 1, :])
            m_pair = jnp.concatenate(ms, axis=1).astype(BF16)
            xp = xs[:, e0 * SSM_HEAD_DIM:(e0 + 2) * SSM_HEAD_DIM]
            rhs = jnp.concatenate([jnp.where(lane2 < SSM_HEAD_DIM, xp, 0.0),
                                   jnp.where(lane2 >= SSM_HEAD_DIM, xp, 0.0)], axis=0)
            y_ref[:, e0 * SSM_HEAD_DIM:(e0 + 2) * SSM_HEAD_DIM] = _dot(m_pair, rhs.astype(BF16))

    for g in range(SSM_GROUPS):
        cols = slice(g * SSM_GROUP_WIDTH, (g + 1) * SSM_GROUP_WIDTH)
        st = st_ref[:, cols]
        c_g = c_in[:, g * SSM_STATE:(g + 1) * SSM_STATE]
        b_g = b_in[:, g * SSM_STATE:(g + 1) * SSM_STATE]
        y_ref[:, cols] += _dot(c_g, st.astype(BF16)) * eld_x[:, cols]
        chunk_state = _dot_tn(b_g, (xs[:, cols] * w_x[:, cols]).astype(BF16))
        st_ref[:, cols] = st * eld_x[CHUNK - 1:CHUNK, cols] + chunk_state

    y = (y_ref[...] + xs * dx_ref[...]) * _silu(z_ref[0])
    for g in range(SSM_GROUPS):
        cols = slice(g * SSM_GROUP_WIDTH, (g + 1) * SSM_GROUP_WIDTH)
        yg = y[:, cols]
        ms = jnp.mean(yg * yg, axis=-1, keepdims=True)
        o_ref[0, :, cols] = yg * lax.rsqrt(ms + EPS) * nw_ref[:, cols]


def _expand_matrix(rows, first_row, n_heads, rep):
    m = np.zeros((rows, n_heads * rep), np.float32)
    for e in range(n_heads):
        m[first_row + e, e * rep:(e + 1) * rep] = 1.0
    return jnp.asarray(m, dtype=BF16)


def _ssd(z, xbc, small, conv_w, conv_b, dt_bias, a_log, d_skip, norm_w, bsz):
    nc = SEQ // CHUNK
    z3 = z.reshape(bsz, SEQ, SSM_WIDTH)
    xbc3 = xbc.reshape(bsz, SEQ, SSM_CONV_DIM)
    small3 = small.reshape(bsz, SEQ, LANES)
    pad_small = lambda v: jnp.pad(v, (SMALL_DT, LANES - SMALL_DT - v.shape[0]))[None, :]
    e64 = _expand_matrix(LANES, SMALL_DT, SSM_HEADS, SSM_HEAD_DIM)
    e128 = _expand_matrix(LANES, SMALL_DT, SSM_HEADS, LANES)
    chunk = lambda n: pl.BlockSpec((1, CHUNK, n), lambda b, c: (b, c, 0))
    const = lambda shape: pl.BlockSpec(shape, lambda b, c: (0, 0))
    out = pl.pallas_call(
        _ssd_kernel,
        grid=(bsz, nc),
        in_specs=[chunk(SSM_WIDTH), chunk(SSM_CONV_DIM), chunk(LANES),
                  const((SSM_CONV, SSM_CONV_DIM)), const((1, SSM_CONV_DIM)),
                  const((1, LANES)), const((1, LANES)),
                  const((1, SSM_WIDTH)), const((1, SSM_WIDTH)),
                  const((LANES, SSM_WIDTH)), const((LANES, SSM_HEADS * LANES))],
        out_specs=chunk(SSM_WIDTH),
        out_shape=jax.ShapeDtypeStruct((bsz, SEQ, SSM_WIDTH), F32),
        scratch_shapes=[pltpu.VMEM((SUBLANES + CHUNK, SSM_CONV_DIM), F32),
                        pltpu.VMEM((SSM_STATE, SSM_WIDTH), F32),
                        pltpu.VMEM((CHUNK, SSM_WIDTH), F32)],
        compiler_params=pltpu.CompilerParams(dimension_semantics=("arbitrary", "arbitrary"),
                                             vmem_limit_bytes=VMEM_LIMIT),
        name="ssd_scan",
    )(z3, xbc3, small3, conv_w, conv_b[None, :], pad_small(dt_bias), pad_small(a_log),
      jnp.repeat(d_skip, SSM_HEAD_DIM)[None, :], norm_w[None, :], e64, e128)
    return out.reshape(bsz * SEQ, SSM_WIDTH)


def _mlstm_kernel(qk_ref, v_ref, og_ref, small_ref, cw_ref, cb_ref, ib_ref, fb_ref, nw_ref,
                  ecf_ref, o_ref, pad_ref, st_ref, m_ref):
    first = pl.program_id(1) == 0
    dh = MLSTM_HEAD_DIM

    @pl.when(first)
    def _():
        st_ref[...] = jnp.zeros(st_ref.shape, F32)
        m_ref[...] = jnp.zeros(m_ref.shape, F32)

    qk = _silu(_chunk_conv(qk_ref[0], cw_ref, cb_ref, pad_ref, first, MLSTM_CONV, CHUNK))
    small = small_ref[0]
    log_i = small + ib_ref[...]
    log_f = -_softplus(-(small + fb_ref[...]))
    tril = _tril_mask(CHUNK)
    tril_b = jnp.where(tril, 1.0, 0.0).astype(BF16)
    cum_f = _dot_exact_lhs(tril_b, log_f)
    cf_b = _dot_exact_rhs(cum_f, ecf_ref[...])
    cf_t = cum_f.T
    li_t = log_i.T
    ones = jnp.ones((CHUNK, dh), BF16)

    for h in range(MLSTM_HEADS):
        cols = slice(h * dh, (h + 1) * dh)
        q = qk[:, h * dh:(h + 1) * dh].astype(BF16)
        k32 = qk[:, MLSTM_WIDTH + h * dh:MLSTM_WIDTH + (h + 1) * dh] * (dh ** -0.5)
        k = k32.astype(BF16)
        v_aug = jnp.concatenate([v_ref[0, :, cols].astype(BF16), ones], axis=1)
        cf_c = cf_b[:, cols]
        cf_r = cf_t[SMALL_F + h:SMALL_F + h + 1, :]
        li_r = li_t[SMALL_I + h:SMALL_I + h + 1, :]
        tot = cf_r[:, CHUNK - 1:CHUNK]
        st = st_ref[h]
        m_st = m_ref[h:h + 1, :]

        d = jnp.where(tril, cf_c - (cf_r - li_r), NEG_INF)
        inter = cf_c + m_st
        m_row = jnp.maximum(jnp.max(d, axis=-1, keepdims=True), inter)
        w_intra = jnp.exp(d - m_row)
        w_inter = jnp.exp(inter - m_row)
        scores = _dot_nt(q, k) * w_intra
        q_state = _dot(q, st.astype(BF16))
        num = _dot(scores.astype(BF16), v_aug[:, :dh]) + w_inter * q_state[:, :dh]
        den = jnp.sum(scores, axis=-1, keepdims=True) + w_inter * q_state[:, dh:]
        hv = num / jnp.maximum(jnp.abs(den), jnp.exp(-m_row))
        hv = hv * lax.rsqrt(jnp.mean(hv * hv, axis=-1, keepdims=True) + EPS) * nw_ref[:, cols]
        o_ref[0, :, cols] = _sigmoid(og_ref[0, :, cols]) * hv

        a_r = tot - cf_r + li_r
        m_loc = jnp.max(a_r, axis=-1, keepdims=True)
        w_r = jnp.exp(a_r - m_loc)
        loc = _dot((k32.T * w_r).astype(BF16), v_aug)
        m_new = jnp.maximum(tot + m_st, m_loc)
        s_old = jnp.exp(tot + m_st - m_new)
        s_new = jnp.exp(m_loc - m_new)
        st_ref[h] = st * jnp.concatenate([s_old, s_old], axis=1) + loc * jnp.concatenate([s_new, s_new], axis=1)
        m_ref[h:h + 1, :] = m_new


def _mlstm(qk, v, og, small, conv_w, conv_b, i_bias, f_bias, norm_w, bsz):
    nc = SEQ // CHUNK
    qk3 = qk.reshape(bsz, SEQ, 2 * MLSTM_WIDTH)
    v3 = v.reshape(bsz, SEQ, MLSTM_WIDTH)
    og3 = og.reshape(bsz, SEQ, MLSTM_WIDTH)
    small3 = small.reshape(bsz, SEQ, LANES)
    pad_at = lambda vec, at: jnp.pad(vec, (at, LANES - at - vec.shape[0]))[None, :]
    ecf = _expand_matrix(LANES, SMALL_F, MLSTM_HEADS, LANES)
    chunk = lambda n: pl.BlockSpec((1, CHUNK, n), lambda b, c: (b, c, 0))
    const = lambda shape: pl.BlockSpec(shape, lambda b, c: (0, 0))
    out = pl.pallas_call(
        _mlstm_kernel,
        grid=(bsz, nc),
        in_specs=[chunk(2 * MLSTM_WIDTH), chunk(MLSTM_WIDTH), chunk(MLSTM_WIDTH), chunk(LANES),
                  const((MLSTM_CONV, 2 * MLSTM_WIDTH)), const((1, 2 * MLSTM_WIDTH)),
                  const((1, LANES)), const((1, LANES)), const((1, MLSTM_WIDTH)),
                  const((LANES, MLSTM_HEADS * LANES))],
        out_specs=chunk(MLSTM_WIDTH),
        out_shape=jax.ShapeDtypeStruct((bsz, SEQ, MLSTM_WIDTH), F32),
        scratch_shapes=[pltpu.VMEM((SUBLANES + CHUNK, 2 * MLSTM_WIDTH), F32),
                        pltpu.VMEM((MLSTM_HEADS, MLSTM_HEAD_DIM, 2 * MLSTM_HEAD_DIM), F32),
                        pltpu.VMEM((SUBLANES, LANES), F32)],
        compiler_params=pltpu.CompilerParams(dimension_semantics=("arbitrary", "arbitrary"),
                                             vmem_limit_bytes=VMEM_LIMIT),
        name="mlstm_scan",
    )(qk3, v3, og3, small3, conv_w, conv_b[None, :], pad_at(i_bias, SMALL_I), pad_at(f_bias, SMALL_F),
      norm_w[None, :], ecf)
    return out.reshape(bsz * SEQ, MLSTM_WIDTH)


FFN_TM = 256


def _out_ffn_kernel(x_ref, ya_ref, ys_ref, ym_ref, wo_ref, nw_ref, wup_ref, cw_ref, cb_ref, wdn_ref,
                    o_ref, pad_ref):
    first = pl.program_id(1) == 0
    mix = _dot(ya_ref[...].astype(BF16), wo_ref[0:ATT_WIDTH, :])
    mix += _dot(ys_ref[...].astype(BF16), wo_ref[ATT_WIDTH:ATT_WIDTH + SSM_WIDTH, :])
    mix += _dot(ym_ref[...].astype(BF16), wo_ref[ATT_WIDTH + SSM_WIDTH:, :])
    x1 = x_ref[...] + mix
    ms = jnp.mean(x1 * x1, axis=-1, keepdims=True)
    h = (x1 * lax.rsqrt(ms + EPS) * nw_ref[...]).astype(BF16)
    gate = _dot(h, wup_ref[:, :FFN_HIDDEN])
    gate = _chunk_conv(gate, cw_ref, cb_ref, pad_ref, first, FFN_CONV, FFN_TM)
    act = 0.5 * gate * (1.0 + lax.erf(gate * (2.0 ** -0.5)))
    act = act * _dot(h, wup_ref[:, FFN_HIDDEN:])
    o_ref[...] = x1 + _dot(act.astype(BF16), wdn_ref[...])


def _out_ffn(x2d, y_att, y_ssm, y_mlstm, w_out, norm_w, w_up, conv_w, conv_b, w_down, bsz):
    t = x2d.shape[0]
    tiles = SEQ // FFN_TM
    row = lambda n: pl.BlockSpec((FFN_TM, n), lambda b, i: (b * tiles + i, 0))
    const = lambda shape: pl.BlockSpec(shape, lambda b, i: (0, 0), pipeline_mode=pl.Buffered(1))
    return pl.pallas_call(
        _out_ffn_kernel,
        grid=(bsz, tiles),
        in_specs=[row(D_MODEL), row(ATT_WIDTH), row(SSM_WIDTH), row(MLSTM_WIDTH),
                  const((D_MIX, D_MODEL)), const((1, D_MODEL)),
                  const((D_MODEL, 2 * FFN_HIDDEN)), const((FFN_CONV, FFN_HIDDEN)),
                  const((1, FFN_HIDDEN)), const((FFN_HIDDEN, D_MODEL))],
        out_specs=row(D_MODEL),
        out_shape=jax.ShapeDtypeStruct((t, D_MODEL), F32),
        scratch_shapes=[pltpu.VMEM((SUBLANES + FFN_TM, FFN_HIDDEN), F32)],
        compiler_params=pltpu.CompilerParams(dimension_semantics=("arbitrary", "arbitrary"),
                                             vmem_limit_bytes=VMEM_LIMIT),
        name="out_proj_convglu",
    )(x2d, y_att, y_ssm, y_mlstm, w_out, norm_w, w_up, conv_w, conv_b, w_down)


def _regroup_in_weights(w):
    offs = np.concatenate([[0], np.cumsum(np.array(PROJ_SIZES))])
    col = lambda i: w[:, int(offs[i]):int(offs[i + 1])]
    small = jnp.concatenate([col(5), col(9), col(10)], axis=1)
    small = jnp.pad(small, ((0, 0), (0, LANES - small.shape[1])))
    return jnp.concatenate([col(0), col(1), col(2), col(3), col(4), col(6), col(7), col(8), small],
                           axis=1).astype(BF16)


def kernel(x, norm1_w, w_in, q_norm_w, k_norm_w, ssm_conv_w, ssm_conv_b, ssm_dt_bias, ssm_a_log, ssm_d, ssm_norm_w, mlstm_conv_w, mlstm_conv_b, mlstm_i_bias, mlstm_f_bias, mlstm_norm_w, w_out, norm2_w, ffn_w_up, ffn_conv_w, ffn_conv_b, ffn_w_down):
    bsz, seq, d_model = x.shape
    assert seq == SEQ and d_model == D_MODEL
    depth = w_in.shape[0]
    x2d = x.reshape(bsz * seq, d_model)
    for l in range(depth):
        qkv, z, xbc, m_qk, m_v, m_o, small = _in_projection(
            x2d, norm1_w[l][None, :], _regroup_in_weights(w_in[l]))
        y_att = _moba_attention(qkv, q_norm_w[l][None, :], k_norm_w[l][None, :], bsz)
        y_ssm = _ssd(z, xbc, small, ssm_conv_w[l], ssm_conv_b[l], ssm_dt_bias[l], ssm_a_log[l],
                     ssm_d[l], ssm_norm_w[l], bsz)
        y_mlstm = _mlstm(m_qk, m_v, m_o, small, mlstm_conv_w[l], mlstm_conv_b[l], mlstm_i_bias[l],
                         mlstm_f_bias[l], mlstm_norm_w[l], bsz)
        x2d = _out_ffn(x2d, y_att, y_ssm, y_mlstm, w_out[l].astype(BF16), norm2_w[l][None, :],
                       ffn_w_up[l].astype(BF16), ffn_conv_w[l], ffn_conv_b[l][None, :],
                       ffn_w_down[l].astype(BF16), bsz)
    return x2d.reshape(bsz, seq, d_model)
```

```python
import numpy as np
import jax
import jax.numpy as jnp
from jax import lax
from jax.experimental import pallas as pl
from jax.experimental.pallas import tpu as pltpu

F32 = jnp.float32
BF16 = jnp.bfloat16

D_MODEL = 1024
SEQ = 2048
D_MIX = 2 * D_MODEL
EPS = 1e-6
NEG_INF = -1e30

ATT_WIDTH = D_MIX // 4
ATT_HEAD_DIM = 64
ATT_HEADS = ATT_WIDTH // ATT_HEAD_DIM
MOBA_BLOCK = 256
MOBA_TOPK = 3
N_MOBA_BLOCKS = SEQ // MOBA_BLOCK

SSM_WIDTH = D_MIX // 2
SSM_HEAD_DIM = 64
SSM_HEADS = SSM_WIDTH // SSM_HEAD_DIM
SSM_GROUPS = 2
SSM_STATE = 128
SSM_CONV = 4
SSM_CHUNK = 128
SSM_CONV_DIM = SSM_WIDTH + 2 * SSM_GROUPS * SSM_STATE
SSM_GROUP_WIDTH = SSM_WIDTH // SSM_GROUPS

MLSTM_WIDTH = D_MIX // 4
MLSTM_HEAD_DIM = 128
MLSTM_HEADS = MLSTM_WIDTH // MLSTM_HEAD_DIM
MLSTM_CONV = 4
MLSTM_CHUNK = 128

FFN_HIDDEN = 11 * D_MODEL // 4
FFN_CONV = 3

PROJ_SIZES = (ATT_WIDTH, ATT_WIDTH, ATT_WIDTH,
              SSM_WIDTH, SSM_CONV_DIM, SSM_HEADS,
              2 * MLSTM_WIDTH, MLSTM_WIDTH, MLSTM_WIDTH,
              MLSTM_HEADS, MLSTM_HEADS)

LANES = 128
SUBLANES = 8
CHUNK = 128
SMALL_DT = 0
SMALL_I = SSM_HEADS
SMALL_F = SSM_HEADS + MLSTM_HEADS
VMEM_LIMIT = 56 * 1024 * 1024


def _dot(a, b):
    return jnp.dot(a, b, preferred_element_type=F32)


def _dot_nt(a, b):
    return lax.dot_general(a, b, (((1,), (1,)), ((), ())), preferred_element_type=F32)


def _dot_tn(a, b):
    return lax.dot_general(a, b, (((0,), (0,)), ((), ())), preferred_element_type=F32)


def _split3(v):
    hi = v.astype(BF16)
    r1 = v - hi.astype(F32)
    mid = r1.astype(BF16)
    lo = (r1 - mid.astype(F32)).astype(BF16)
    return hi, mid, lo


def _dot_exact_lhs(e, v):
    hi, mid, lo = _split3(v)
    return _dot(e, hi) + _dot(e, mid) + _dot(e, lo)


def _dot_exact_rhs(v, e):
    hi, mid, lo = _split3(v)
    return _dot(hi, e) + _dot(mid, e) + _dot(lo, e)


def _sigmoid(x):
    return 1.0 / (1.0 + jnp.exp(-x))


def _silu(x):
    return x * _sigmoid(x)


def _softplus(x):
    return jnp.maximum(x, 0.0) + jnp.log1p(jnp.exp(-jnp.abs(x)))


def _tril_mask(n):
    r = lax.broadcasted_iota(jnp.int32, (n, n), 0)
    c = lax.broadcasted_iota(jnp.int32, (n, n), 1)
    return c <= r


IN_TM = 256
IN_GROUPS = (3 * ATT_WIDTH, SSM_WIDTH, SSM_CONV_DIM, 2 * MLSTM_WIDTH, MLSTM_WIDTH, MLSTM_WIDTH, LANES)
IN_COLS = sum(IN_GROUPS)
DOT_N = 512


def _inproj_kernel(x_ref, nw_ref, w_ref, *out_refs):
    x = x_ref[...]
    ms = jnp.mean(x * x, axis=-1, keepdims=True)
    h = (x * lax.rsqrt(ms + EPS) * nw_ref[...]).astype(BF16)
    off = 0
    for ref in out_refs:
        n = ref.shape[-1]
        for c in range(0, n, DOT_N):
            cw = min(DOT_N, n - c)
            ref[:, c:c + cw] = _dot(h, w_ref[:, off + c:off + c + cw]).astype(ref.dtype)
        off += n


def _in_projection(x2d, norm_w, w_cat):
    t = x2d.shape[0]
    const = lambda i: (0, 0)
    row = lambda i: (i, 0)
    return pl.pallas_call(
        _inproj_kernel,
        grid=(t // IN_TM,),
        in_specs=[pl.BlockSpec((IN_TM, D_MODEL), row),
                  pl.BlockSpec((1, D_MODEL), const),
                  pl.BlockSpec((D_MODEL, IN_COLS), const)],
        out_specs=[pl.BlockSpec((IN_TM, n), row) for n in IN_GROUPS],
        out_shape=[jax.ShapeDtypeStruct((t, n), F32) for n in IN_GROUPS],
        compiler_params=pltpu.CompilerParams(dimension_semantics=("arbitrary",),
                                             vmem_limit_bytes=VMEM_LIMIT),
        name="in_projection",
    )(x2d, norm_w, w_cat)


HEADS_PER_STEP = LANES // ATT_HEAD_DIM
BIAS_LANE = (ATT_HEAD_DIM, 0)


def _moba_kernel(q_ref, k_ref, v_ref, qw_ref, kw_ref, o_ref, qa_ref, ka_ref, vb_ref, s_ref):
    nb = N_MOBA_BLOCKS
    blk = MOBA_BLOCK
    lane = lax.broadcasted_iota(jnp.int32, (1, LANES), 1)
    head_lanes = [(lane >= hh * ATT_HEAD_DIM) & (lane < (hh + 1) * ATT_HEAD_DIM)
                  for hh in range(HEADS_PER_STEP)]
    same_head = jnp.where(lax.broadcasted_iota(jnp.int32, (LANES, LANES), 0) // ATT_HEAD_DIM
                          == lax.broadcasted_iota(jnp.int32, (LANES, LANES), 1) // ATT_HEAD_DIM,
                          1.0, 0.0).astype(BF16)
    pool = jnp.where(lax.broadcasted_iota(jnp.int32, (nb, SEQ), 1) // blk
                     == lax.broadcasted_iota(jnp.int32, (nb, SEQ), 0),
                     1.0 / blk, 0.0).astype(BF16)

    def head_rmsnorm(x, w):
        ss = _dot_exact_rhs(x * x, same_head)
        return x * lax.rsqrt(ss * (1.0 / ATT_HEAD_DIM) + EPS) * w

    qn = head_rmsnorm(q_ref[0], qw_ref[...])
    kn = head_rmsnorm(k_ref[0], kw_ref[...])
    v = v_ref[0]
    k_mean = _dot_exact_lhs(pool, kn)

    cand = lax.broadcasted_iota(jnp.int32, (nb, SEQ), 0)
    own = lax.broadcasted_iota(jnp.int32, (nb, SEQ), 1) // blk
    valid = cand < own
    bias_rows = {}
    for hh in range(HEADS_PER_STEP):
        gate = lax.dot_general(jnp.where(head_lanes[hh], k_mean, 0.0), qn, (((1,), (1,)), ((), ())),
                               precision=lax.Precision.HIGHEST, preferred_element_type=F32)
        gm = jnp.where(valid, gate, NEG_INF)
        rank = jnp.zeros((nb, SEQ), jnp.int32)
        for j in range(nb):
            gj = gm[j:j + 1, :]
            ahead = (gj > gm) | ((gj == gm) & (j < cand))
            rank = rank + ahead.astype(jnp.int32)
        allowed = (valid & (rank < MOBA_TOPK)) | (cand == own)
        bias_rows[BIAS_LANE[hh]] = jnp.where(allowed, 0.0, NEG_INF)
    zeros = jnp.zeros((ATT_HEAD_DIM - nb, SEQ), F32)
    bias_t = jnp.concatenate([bias_rows[0], zeros, bias_rows[ATT_HEAD_DIM], zeros], axis=0)
    bias = bias_t.T

    row_blk = lax.broadcasted_iota(jnp.int32, (SEQ, LANES), 0) // blk
    lane_full = lax.broadcasted_iota(jnp.int32, (SEQ, LANES), 1)
    scale = ATT_HEAD_DIM ** -0.5
    for hh in range(HEADS_PER_STEP):
        b0 = BIAS_LANE[hh]
        route = (lane_full >= b0) & (lane_full < b0 + nb)
        onehot = jnp.where(route & (lane_full - b0 == row_blk), 1.0, 0.0)
        qa_ref[hh] = jnp.where(head_lanes[hh], qn * scale, jnp.where(route, bias, 0.0)).astype(BF16)
        ka_ref[hh] = jnp.where(head_lanes[hh], kn, onehot).astype(BF16)
        vb_ref[hh] = jnp.where(head_lanes[hh], v, 1.0).astype(BF16)

    causal = _tril_mask(blk)
    for qb in range(nb):
        rows = slice(qb * blk, (qb + 1) * blk)
        accs = []
        for hh in range(HEADS_PER_STEP):
            qa = qa_ref[hh, rows, :]
            mx = None
            for kj in range(qb + 1):
                s = _dot_nt(qa, ka_ref[hh, kj * blk:(kj + 1) * blk, :])
                if kj == qb:
                    s = jnp.where(causal, s, NEG_INF)
                s_ref[hh, :, kj * blk:(kj + 1) * blk] = s
                t = jnp.maximum(s[:, :LANES], s[:, LANES:])
                mx = t if mx is None else jnp.maximum(mx, t)
            m = jnp.max(mx, axis=-1, keepdims=True)
            acc = None
            for kj in range(qb + 1):
                p = jnp.exp(s_ref[hh, :, kj * blk:(kj + 1) * blk] - m).astype(BF16)
                pv = _dot(p, vb_ref[hh, kj * blk:(kj + 1) * blk, :])
                acc = pv if acc is None else acc + pv
            accs.append(acc)
        num = jnp.where(head_lanes[0], accs[0], accs[1])
        den = jnp.where(head_lanes[0], pltpu.roll(accs[0], ATT_HEAD_DIM, 1),
                        pltpu.roll(accs[1], ATT_HEAD_DIM, 1))
        o_ref[0, rows, :] = num / den


def _moba_attention(qkv, q_norm_w, k_norm_w, bsz):
    qkv3 = qkv.reshape(bsz, SEQ, 3 * ATT_WIDTH)
    n_pairs = ATT_WIDTH // LANES
    blk = lambda off: pl.BlockSpec((1, SEQ, LANES), lambda b, h: (b, 0, off + h))
    const = lambda b, h: (0, 0)
    tile_w = lambda w: jnp.tile(w, (1, HEADS_PER_STEP))
    out = pl.pallas_call(
        _moba_kernel,
        grid=(bsz, n_pairs),
        in_specs=[blk(0), blk(n_pairs), blk(2 * n_pairs),
                  pl.BlockSpec((1, LANES), const),
                  pl.BlockSpec((1, LANES), const)],
        out_specs=pl.BlockSpec((1, SEQ, LANES), lambda b, h: (b, 0, h)),
        out_shape=jax.ShapeDtypeStruct((bsz, SEQ, ATT_WIDTH), F32),
        scratch_shapes=[pltpu.VMEM((HEADS_PER_STEP, SEQ, LANES), BF16),
                        pltpu.VMEM((HEADS_PER_STEP, SEQ, LANES), BF16),
                        pltpu.VMEM((HEADS_PER_STEP, SEQ, LANES), BF16),
                        pltpu.VMEM((HEADS_PER_STEP, MOBA_BLOCK, SEQ), F32)],
        compiler_params=pltpu.CompilerParams(dimension_semantics=("arbitrary", "arbitrary"),
                                             vmem_limit_bytes=VMEM_LIMIT),
        name="moba_attention",
    )(qkv3, qkv3, qkv3, tile_w(q_norm_w), tile_w(k_norm_w))
    return out.reshape(bsz * SEQ, ATT_WIDTH)


def _chunk_conv(x, w_ref, b_ref, pad_ref, first, width, rows):
    @pl.when(first)
    def _():
        pad_ref[0:SUBLANES, :] = jnp.zeros((SUBLANES, pad_ref.shape[1]), F32)

    pad_ref[SUBLANES:SUBLANES + rows, :] = x
    acc = b_ref[...] + w_ref[width - 1:width, :] * x
    for j in range(width - 1):
        s = SUBLANES - (width - 1) + j
        acc = acc + w_ref[j:j + 1, :] * pad_ref[s:s + rows, :]
    pad_ref[0:SUBLANES, :] = pad_ref[rows:rows + SUBLANES, :]
    return acc


def _ssd_kernel(z_ref, xbc_ref, small_ref, cw_ref, cb_ref, dtb_ref, alog_ref, dx_ref, nw_ref,
                e64_ref, e128_ref, o_ref, pad_ref, st_ref, y_ref):
    first = pl.program_id(1) == 0

    @pl.when(first)
    def _():
        st_ref[...] = jnp.zeros(st_ref.shape, F32)

    xbc = _silu(_chunk_conv(xbc_ref[0], cw_ref, cb_ref, pad_ref, first, SSM_CONV, CHUNK))
    xs = xbc[:, :SSM_WIDTH]
    gn = SSM_GROUPS * SSM_STATE
    b_in = xbc[:, SSM_WIDTH:SSM_WIDTH + gn].astype(BF16)
    c_in = xbc[:, SSM_WIDTH + gn:].astype(BF16)

    lane = lax.broadcasted_iota(jnp.int32, (1, LANES), 1)
    head_lane = lane < SSM_HEADS
    dt = _softplus(small_ref[0] + dtb_ref[...])
    a = jnp.where(head_lane, -jnp.exp(alog_ref[...]), 0.0)
    tril = _tril_mask(CHUNK)
    tril_b = jnp.where(tril, 1.0, 0.0).astype(BF16)
    ld = _dot_exact_lhs(tril_b, dt * a)
    ld_t = ld.T
    dt_t = dt.T
    eld = jnp.exp(ld)
    w_state = dt * jnp.exp(ld[CHUNK - 1:CHUNK, :] - ld)
    e64 = e64_ref[...]
    eld_x = _dot_exact_rhs(eld, e64)
    w_x = _dot_exact_rhs(w_state, e64)
    ld_b = _dot_exact_rhs(ld, e128_ref[...])

    lane2 = lax.broadcasted_iota(jnp.int32, (CHUNK, LANES), 1)
    hpg = SSM_HEADS // SSM_GROUPS
    for g in range(SSM_GROUPS):
        cb = _dot_nt(c_in[:, g * SSM_STATE:(g + 1) * SSM_STATE],
                     b_in[:, g * SSM_STATE:(g + 1) * SSM_STATE])
        for pair in range(hpg // 2):
            e0 = g * hpg + 2 * pair
            ms = []
            for e in (e0, e0 + 1):
                seg = ld_b[:, e * LANES:(e + 1) * LANES] - ld_t[e:e + 1, :]
                dec = jnp.exp(jnp.where(tril, seg, NEG_INF))
                ms.append(cb * dec * dt_t[e:e + 1, :])
            m_pair = jnp.concatenate(ms, axis=1).astype(BF16)
            xp = xs[:, e0 * SSM_HEAD_DIM:(e0 + 2) * SSM_HEAD_DIM]
            rhs = jnp.concatenate([jnp.where(lane2 < SSM_HEAD_DIM, xp, 0.0),
                                   jnp.where(lane2 >= SSM_HEAD_DIM, xp, 0.0)], axis=0)
            y_ref[:, e0 * SSM_HEAD_DIM:(e0 + 2) * SSM_HEAD_DIM] = _dot(m_pair, rhs.astype(BF16))

    for g in range(SSM_GROUPS):
        cols = slice(g * SSM_GROUP_WIDTH, (g + 1) * SSM_GROUP_WIDTH)
        st = st_ref[:, cols]
        c_g = c_in[:, g * SSM_STATE:(g + 1) * SSM_STATE]
        b_g = b_in[:, g * SSM_STATE:(g + 1) * SSM_STATE]
        y_ref[:, cols] += _dot(c_g, st.astype(BF16)) * eld_x[:, cols]
        chunk_state = _dot_tn(b_g, (xs[:, cols] * w_x[:, cols]).astype(BF16))
        st_ref[:, cols] = st * eld_x[CHUNK - 1:CHUNK, cols] + chunk_state

    y = (y_ref[...] + xs * dx_ref[...]) * _silu(z_ref[0])
    for g in range(SSM_GROUPS):
        cols = slice(g * SSM_GROUP_WIDTH, (g + 1) * SSM_GROUP_WIDTH)
        yg = y[:, cols]
        ms = jnp.mean(yg * yg, axis=-1, keepdims=True)
        o_ref[0, :, cols] = yg * lax.rsqrt(ms + EPS) * nw_ref[:, cols]


def _expand_matrix(rows, first_row, n_heads, rep):
    m = np.zeros((rows, n_heads * rep), np.float32)
    for e in range(n_heads):
        m[first_row + e, e * rep:(e + 1) * rep] = 1.0
    return jnp.asarray(m, dtype=BF16)


def _ssd(z, xbc, small, conv_w, conv_b, dt_bias, a_log, d_skip, norm_w, bsz):
    nc = SEQ // CHUNK
    z3 = z.reshape(bsz, SEQ, SSM_WIDTH)
    xbc3 = xbc.reshape(bsz, SEQ, SSM_CONV_DIM)
    small3 = small.reshape(bsz, SEQ, LANES)
    pad_small = lambda v: jnp.pad(v, (SMALL_DT, LANES - SMALL_DT - v.shape[0]))[None, :]
    e64 = _expand_matrix(LANES, SMALL_DT, SSM_HEADS, SSM_HEAD_DIM)
    e128 = _expand_matrix(LANES, SMALL_DT, SSM_HEADS, LANES)
    chunk = lambda n: pl.BlockSpec((1, CHUNK, n), lambda b, c: (b, c, 0))
    const = lambda shape: pl.BlockSpec(shape, lambda b, c: (0, 0))
    out = pl.pallas_call(
        _ssd_kernel,
        grid=(bsz, nc),
        in_specs=[chunk(SSM_WIDTH), chunk(SSM_CONV_DIM), chunk(LANES),
                  const((SSM_CONV, SSM_CONV_DIM)), const((1, SSM_CONV_DIM)),
                  const((1, LANES)), const((1, LANES)),
                  const((1, SSM_WIDTH)), const((1, SSM_WIDTH)),
                  const((LANES, SSM_WIDTH)), const((LANES, SSM_HEADS * LANES))],
        out_specs=chunk(SSM_WIDTH),
        out_shape=jax.ShapeDtypeStruct((bsz, SEQ, SSM_WIDTH), F32),
        scratch_shapes=[pltpu.VMEM((SUBLANES + CHUNK, SSM_CONV_DIM), F32),
                        pltpu.VMEM((SSM_STATE, SSM_WIDTH), F32),
                        pltpu.VMEM((CHUNK, SSM_WIDTH), F32)],
        compiler_params=pltpu.CompilerParams(dimension_semantics=("arbitrary", "arbitrary"),
                                             vmem_limit_bytes=VMEM_LIMIT),
        name="ssd_scan",
    )(z3, xbc3, small3, conv_w, conv_b[None, :], pad_small(dt_bias), pad_small(a_log),
      jnp.repeat(d_skip, SSM_HEAD_DIM)[None, :], norm_w[None, :], e64, e128)
    return out.reshape(bsz * SEQ, SSM_WIDTH)


def _mlstm_kernel(qk_ref, v_ref, og_ref, small_ref, cw_ref, cb_ref, ib_ref, fb_ref, nw_ref,
                  ecf_ref, o_ref, pad_ref, st_ref, m_ref):
    first = pl.program_id(1) == 0
    dh = MLSTM_HEAD_DIM

    @pl.when(first)
    def _():
        st_ref[...] = jnp.zeros(st_ref.shape, F32)
        m_ref[...] = jnp.zeros(m_ref.shape, F32)

    qk = _silu(_chunk_conv(qk_ref[0], cw_ref, cb_ref, pad_ref, first, MLSTM_CONV, CHUNK))
    small = small_ref[0]
    log_i = small + ib_ref[...]
    log_f = -_softplus(-(small + fb_ref[...]))
    tril = _tril_mask(CHUNK)
    tril_b = jnp.where(tril, 1.0, 0.0).astype(BF16)
    cum_f = _dot_exact_lhs(tril_b, log_f)
    cf_b = _dot_exact_rhs(cum_f, ecf_ref[...])
    cf_t = cum_f.T
    li_t = log_i.T
    ones = jnp.ones((CHUNK, dh), BF16)

    for h in range(MLSTM_HEADS):
        cols = slice(h * dh, (h + 1) * dh)
        q = qk[:, h * dh:(h + 1) * dh].astype(BF16)
        k32 = qk[:, MLSTM_WIDTH + h * dh:MLSTM_WIDTH + (h + 1) * dh] * (dh ** -0.5)
        k = k32.astype(BF16)
        v_aug = jnp.concatenate([v_ref[0, :, cols].astype(BF16), ones], axis=1)
        cf_c = cf_b[:, cols]
        cf_r = cf_t[SMALL_F + h:SMALL_F + h + 1, :]
        li_r = li_t[SMALL_I + h:SMALL_I + h + 1, :]
        tot = cf_r[:, CHUNK - 1:CHUNK]
        st = st_ref[h]
        m_st = m_ref[h:h + 1, :]

        d = jnp.where(tril, cf_c - (cf_r - li_r), NEG_INF)
        inter = cf_c + m_st
        m_row = jnp.maximum(jnp.max(d, axis=-1, keepdims=True), inter)
        w_intra = jnp.exp(d - m_row)
        w_inter = jnp.exp(inter - m_row)
        scores = _dot_nt(q, k) * w_intra
        q_state = _dot(q, st.astype(BF16))
        num = _dot(scores.astype(BF16), v_aug[:, :dh]) + w_inter * q_state[:, :dh]
        den = jnp.sum(scores, axis=-1, keepdims=True) + w_inter * q_state[:, dh:]
        hv = num / jnp.maximum(jnp.abs(den), jnp.exp(-m_row))
        hv = hv * lax.rsqrt(jnp.mean(hv * hv, axis=-1, keepdims=True) + EPS) * nw_ref[:, cols]
        o_ref[0, :, cols] = _sigmoid(og_ref[0, :, cols]) * hv

        a_r = tot - cf_r + li_r
        m_loc = jnp.max(a_r, axis=-1, keepdims=True)
        w_r = jnp.exp(a_r - m_loc)
        loc = _dot((k32.T * w_r).astype(BF16), v_aug)
        m_new = jnp.maximum(tot + m_st, m_loc)
        s_old = jnp.exp(tot + m_st - m_new)
        s_new = jnp.exp(m_loc - m_new)
        st_ref[h] = st * jnp.concatenate([s_old, s_old], axis=1) + loc * jnp.concatenate([s_new, s_new], axis=1)
        m_ref[h:h + 1, :] = m_new


def _mlstm(qk, v, og, small, conv_w, conv_b, i_bias, f_bias, norm_w, bsz):
    nc = SEQ // CHUNK
    qk3 = qk.reshape(bsz, SEQ, 2 * MLSTM_WIDTH)
    v3 = v.reshape(bsz, SEQ, MLSTM_WIDTH)
    og3 = og.reshape(bsz, SEQ, MLSTM_WIDTH)
    small3 = small.reshape(bsz, SEQ, LANES)
    pad_at = lambda vec, at: jnp.pad(vec, (at, LANES - at - vec.shape[0]))[None, :]
    ecf = _expand_matrix(LANES, SMALL_F, MLSTM_HEADS, LANES)
    chunk = lambda n: pl.BlockSpec((1, CHUNK, n), lambda b, c: (b, c, 0))
    const = lambda shape: pl.BlockSpec(shape, lambda b, c: (0, 0))
    out = pl.pallas_call(
        _mlstm_kernel,
        grid=(bsz, nc),
        in_specs=[chunk(2 * MLSTM_WIDTH), chunk(MLSTM_WIDTH), chunk(MLSTM_WIDTH), chunk(LANES),
                  const((MLSTM_CONV, 2 * MLSTM_WIDTH)), const((1, 2 * MLSTM_WIDTH)),
                  const((1, LANES)), const((1, LANES)), const((1, MLSTM_WIDTH)),
                  const((LANES, MLSTM_HEADS * LANES))],
        out_specs=chunk(MLSTM_WIDTH),
        out_shape=jax.ShapeDtypeStruct((bsz, SEQ, MLSTM_WIDTH), F32),
        scratch_shapes=[pltpu.VMEM((SUBLANES + CHUNK, 2 * MLSTM_WIDTH), F32),
                        pltpu.VMEM((MLSTM_HEADS, MLSTM_HEAD_DIM, 2 * MLSTM_HEAD_DIM), F32),
                        pltpu.VMEM((SUBLANES, LANES), F32)],
        compiler_params=pltpu.CompilerParams(dimension_semantics=("arbitrary", "arbitrary"),
                                             vmem_limit_bytes=VMEM_LIMIT),
        name="mlstm_scan",
    )(qk3, v3, og3, small3, conv_w, conv_b[None, :], pad_at(i_bias, SMALL_I), pad_at(f_bias, SMALL_F),
      norm_w[None, :], ecf)
    return out.reshape(bsz * SEQ, MLSTM_WIDTH)


FFN_TM = 256


def _out_ffn_kernel(x_ref, ya_ref, ys_ref, ym_ref, wo_ref, nw_ref, wup_ref, cw_ref, cb_ref, wdn_ref,
                    o_ref, pad_ref):
    first = pl.program_id(1) == 0
    mix = _dot(ya_ref[...].astype(BF16), wo_ref[0:ATT_WIDTH, :])
    mix += _dot(ys_ref[...].astype(BF16), wo_ref[ATT_WIDTH:ATT_WIDTH + SSM_WIDTH, :])
    mix += _dot(ym_ref[...].astype(BF16), wo_ref[ATT_WIDTH + SSM_WIDTH:, :])
    x1 = x_ref[...] + mix
    ms = jnp.mean(x1 * x1, axis=-1, keepdims=True)
    h = (x1 * lax.rsqrt(ms + EPS) * nw_ref[...]).astype(BF16)
    gate = _dot(h, wup_ref[:, :FFN_HIDDEN])
    gate = _chunk_conv(gate, cw_ref, cb_ref, pad_ref, first, FFN_CONV, FFN_TM)
    act = 0.5 * gate * (1.0 + lax.erf(gate * (2.0 ** -0.5)))
    act = act * _dot(h, wup_ref[:, FFN_HIDDEN:])
    o_ref[...] = x1 + _dot(act.astype(BF16), wdn_ref[...])


def _out_ffn(x2d, y_att, y_ssm, y_mlstm, w_out, norm_w, w_up, conv_w, conv_b, w_down, bsz):
    t = x2d.shape[0]
    tiles = SEQ // FFN_TM
    row = lambda n: pl.BlockSpec((FFN_TM, n), lambda b, i: (b * tiles + i, 0))
    const = lambda shape: pl.BlockSpec(shape, lambda b, i: (0, 0), pipeline_mode=pl.Buffered(1))
    return pl.pallas_call(
        _out_ffn_kernel,
        grid=(bsz, tiles),
        in_specs=[row(D_MODEL), row(ATT_WIDTH), row(SSM_WIDTH), row(MLSTM_WIDTH),
                  const((D_MIX, D_MODEL)), const((1, D_MODEL)),
                  const((D_MODEL, 2 * FFN_HIDDEN)), const((FFN_CONV, FFN_HIDDEN)),
                  const((1, FFN_HIDDEN)), const((FFN_HIDDEN, D_MODEL))],
        out_specs=row(D_MODEL),
        out_shape=jax.ShapeDtypeStruct((t, D_MODEL), F32),
        scratch_shapes=[pltpu.VMEM((SUBLANES + FFN_TM, FFN_HIDDEN), F32)],
        compiler_params=pltpu.CompilerParams(dimension_semantics=("arbitrary", "arbitrary"),
                                             vmem_limit_bytes=VMEM_LIMIT),
        name="out_proj_convglu",
    )(x2d, y_att, y_ssm, y_mlstm, w_out, norm_w, w_up, conv_w, conv_b, w_down)


def _regroup_in_weights(w):
    offs = np.concatenate([[0], np.cumsum(np.array(PROJ_SIZES))])
    col = lambda i: w[:, int(offs[i]):int(offs[i + 1])]
    small = jnp.concatenate([col(5), col(9), col(10)], axis=1)
    small = jnp.pad(small, ((0, 0), (0, LANES - small.shape[1])))
    return jnp.concatenate([col(0), col(1), col(2), col(3), col(4), col(6), col(7), col(8), small],
                           axis=1).astype(BF16)


def kernel(x, norm1_w, w_in, q_norm_w, k_norm_w, ssm_conv_w, ssm_conv_b, ssm_dt_bias, ssm_a_log, ssm_d, ssm_norm_w, mlstm_conv_w, mlstm_conv_b, mlstm_i_bias, mlstm_f_bias, mlstm_norm_w, w_out, norm2_w, ffn_w_up, ffn_conv_w, ffn_conv_b, ffn_w_down):
    bsz, seq, d_model = x.shape
    assert seq == SEQ and d_model == D_MODEL
    depth = w_in.shape[0]
    x2d = x.reshape(bsz * seq, d_model)
    for l in range(depth):
        qkv, z, xbc, m_qk, m_v, m_o, small = _in_projection(
            x2d, norm1_w[l][None, :], _regroup_in_weights(w_in[l]))
        y_att = _moba_attention(qkv, q_norm_w[l][None, :], k_norm_w[l][None, :], bsz)
        y_ssm = _ssd(z, xbc, small, ssm_conv_w[l], ssm_conv_b[l], ssm_dt_bias[l], ssm_a_log[l],
                     ssm_d[l], ssm_norm_w[l], bsz)
        y_mlstm = _mlstm(m_qk, m_v, m_o, small, mlstm_conv_w[l], mlstm_conv_b[l], mlstm_i_bias[l],
                         mlstm_f_bias[l], mlstm_norm_w[l], bsz)
        x2d = _out_ffn(x2d, y_att, y_ssm, y_mlstm, w_out[l].astype(BF16), norm2_w[l][None, :],
                       ffn_w_up[l].astype(BF16), ffn_conv_w[l], ffn_conv_b[l][None, :],
                       ffn_w_down[l].astype(BF16), bsz)
    return x2d.reshape(bsz, seq, d_model)
```

```python
import numpy as np
import jax
import jax.numpy as jnp
from jax import lax
from jax.experimental import pallas as pl
from jax.experimental.pallas import tpu as pltpu

F32 = jnp.float32
BF16 = jnp.bfloat16

D_MODEL = 1024
SEQ = 2048
D_MIX = 2 * D_MODEL
EPS = 1e-6
NEG_INF = -1e30

ATT_WIDTH = D_MIX // 4
ATT_HEAD_DIM = 64
ATT_HEADS = ATT_WIDTH // ATT_HEAD_DIM
MOBA_BLOCK = 256
MOBA_TOPK = 3
N_MOBA_BLOCKS = SEQ // MOBA_BLOCK

SSM_WIDTH = D_MIX // 2
SSM_HEAD_DIM = 64
SSM_HEADS = SSM_WIDTH // SSM_HEAD_DIM
SSM_GROUPS = 2
SSM_STATE = 128
SSM_CONV = 4
SSM_CHUNK = 128
SSM_CONV_DIM = SSM_WIDTH + 2 * SSM_GROUPS * SSM_STATE
SSM_GROUP_WIDTH = SSM_WIDTH // SSM_GROUPS

MLSTM_WIDTH = D_MIX // 4
MLSTM_HEAD_DIM = 128
MLSTM_HEADS = MLSTM_WIDTH // MLSTM_HEAD_DIM
MLSTM_CONV = 4
MLSTM_CHUNK = 128

FFN_HIDDEN = 11 * D_MODEL // 4
FFN_CONV = 3

PROJ_SIZES = (ATT_WIDTH, ATT_WIDTH, ATT_WIDTH,
              SSM_WIDTH, SSM_CONV_DIM, SSM_HEADS,
              2 * MLSTM_WIDTH, MLSTM_WIDTH, MLSTM_WIDTH,
              MLSTM_HEADS, MLSTM_HEADS)

LANES = 128
SUBLANES = 8
CHUNK = 128
SMALL_DT = 0
SMALL_I = SSM_HEADS
SMALL_F = SSM_HEADS + MLSTM_HEADS
VMEM_LIMIT = 56 * 1024 * 1024
EXPAND_TERMS = 2
MIX_DTYPE = BF16


def _dot(a, b):
    return jnp.dot(a, b, preferred_element_type=F32)


def _dot_nt(a, b):
    return lax.dot_general(a, b, (((1,), (1,)), ((), ())), preferred_element_type=F32)


def _dot_tn(a, b):
    return lax.dot_general(a, b, (((0,), (0,)), ((), ())), preferred_element_type=F32)


def _split(v, terms):
    out = []
    for _ in range(terms - 1):
        part = v.astype(BF16)
        out.append(part)
        v = v - part.astype(F32)
    out.append(v.astype(BF16))
    return out


def _dot_exact_lhs(e, v, terms=3):
    parts = _split(v, terms)
    acc = _dot(e, parts[0])
    for part in parts[1:]:
        acc = acc + _dot(e, part)
    return acc


def _dot_exact_rhs(v, e, terms=3):
    parts = _split(v, terms)
    acc = _dot(parts[0], e)
    for part in parts[1:]:
        acc = acc + _dot(part, e)
    return acc


def _sigmoid(x):
    return 1.0 / (1.0 + jnp.exp(-x))


def _silu(x):
    return x * _sigmoid(x)


def _softplus(x):
    return jnp.maximum(x, 0.0) + jnp.log1p(jnp.exp(-jnp.abs(x)))


def _tril_mask(n):
    r = lax.broadcasted_iota(jnp.int32, (n, n), 0)
    c = lax.broadcasted_iota(jnp.int32, (n, n), 1)
    return c <= r


IN_TM = 512
IN_GROUPS = (3 * ATT_WIDTH, SSM_WIDTH, SSM_CONV_DIM, 2 * MLSTM_WIDTH, MLSTM_WIDTH, MLSTM_WIDTH, LANES)
IN_COLS = sum(IN_GROUPS)
DOT_N = 512


def _inproj_kernel(x_ref, nw_ref, w_ref, *out_refs):
    x = x_ref[...]
    ms = jnp.mean(x * x, axis=-1, keepdims=True)
    h = (x * lax.rsqrt(ms + EPS) * nw_ref[...]).astype(BF16)
    off = 0
    for ref in out_refs:
        n = ref.shape[-1]
        for c in range(0, n, DOT_N):
            cw = min(DOT_N, n - c)
            ref[:, c:c + cw] = _dot(h, w_ref[:, off + c:off + c + cw]).astype(ref.dtype)
        off += n


def _in_projection(x2d, norm_w, w_cat):
    t = x2d.shape[0]
    const = lambda i: (0, 0)
    row = lambda i: (i, 0)
    return pl.pallas_call(
        _inproj_kernel,
        grid=(t // IN_TM,),
        in_specs=[pl.BlockSpec((IN_TM, D_MODEL), row),
                  pl.BlockSpec((1, D_MODEL), const),
                  pl.BlockSpec((D_MODEL, IN_COLS), const, pipeline_mode=pl.Buffered(1))],
        out_specs=[pl.BlockSpec((IN_TM, n), row) for n in IN_GROUPS],
        out_shape=[jax.ShapeDtypeStruct((t, n), F32) for n in IN_GROUPS],
        compiler_params=pltpu.CompilerParams(dimension_semantics=("arbitrary",),
                                             vmem_limit_bytes=VMEM_LIMIT),
        name="in_projection",
    )(x2d, norm_w, w_cat)


HEADS_PER_STEP = LANES // ATT_HEAD_DIM
BIAS_LANE = (ATT_HEAD_DIM, 0)


def _moba_kernel(q_ref, k_ref, v_ref, qw_ref, kw_ref, o_ref, qa_ref, ka_ref, vb_ref, s_ref):
    nb = N_MOBA_BLOCKS
    blk = MOBA_BLOCK
    lane = lax.broadcasted_iota(jnp.int32, (1, LANES), 1)
    head_lanes = [(lane >= hh * ATT_HEAD_DIM) & (lane < (hh + 1) * ATT_HEAD_DIM)
                  for hh in range(HEADS_PER_STEP)]
    same_head = jnp.where(lax.broadcasted_iota(jnp.int32, (LANES, LANES), 0) // ATT_HEAD_DIM
                          == lax.broadcasted_iota(jnp.int32, (LANES, LANES), 1) // ATT_HEAD_DIM,
                          1.0, 0.0).astype(BF16)

    def head_rmsnorm(x, w):
        ss = _dot_exact_rhs(x * x, same_head, terms=2)
        return x * lax.rsqrt(ss * (1.0 / ATT_HEAD_DIM) + EPS) * w

    qn = head_rmsnorm(q_ref[0], qw_ref[...])
    kn = head_rmsnorm(k_ref[0], kw_ref[...])
    v = v_ref[0]
    k_mean = jnp.sum(kn.reshape(nb, blk, LANES), axis=1) * (1.0 / blk)

    q_hi, q_lo = _split(qn, 2)
    k_hi, k_lo = [], []
    for hh in range(HEADS_PER_STEP):
        km = jnp.where(head_lanes[hh], k_mean, 0.0)
        hi = km.astype(BF16).astype(F32)
        k_hi.append(hi)
        k_lo.append(km - hi)
    g_hi = _dot_nt(jnp.concatenate([k_hi[0], k_lo[0], k_hi[1], k_lo[1]], axis=0).astype(BF16), q_hi)
    g_lo = _dot_nt(jnp.concatenate(k_hi, axis=0).astype(BF16), q_lo)

    cand = lax.broadcasted_iota(jnp.int32, (nb, SEQ), 0)
    own = lax.broadcasted_iota(jnp.int32, (nb, SEQ), 1) // blk
    valid = cand < own
    bias_rows = {}
    for hh in range(HEADS_PER_STEP):
        gate = (g_hi[2 * hh * nb:(2 * hh + 1) * nb] + g_hi[(2 * hh + 1) * nb:(2 * hh + 2) * nb]
                + g_lo[hh * nb:(hh + 1) * nb])
        gm = jnp.where(valid, gate, NEG_INF)
        rank = jnp.zeros((nb, SEQ), jnp.int32)
        for j in range(nb):
            gj = gm[j:j + 1, :]
            ahead = (gj > gm) | ((gj == gm) & (j < cand))
            rank = rank + ahead.astype(jnp.int32)
        allowed = (valid & (rank < MOBA_TOPK)) | (cand == own)
        bias_rows[BIAS_LANE[hh]] = jnp.where(allowed, 0.0, NEG_INF)
    zeros = jnp.zeros((ATT_HEAD_DIM - nb, SEQ), F32)
    bias_t = jnp.concatenate([bias_rows[0], zeros, bias_rows[ATT_HEAD_DIM], zeros], axis=0)
    bias = bias_t.T

    row_blk = lax.broadcasted_iota(jnp.int32, (SEQ, LANES), 0) // blk
    lane_full = lax.broadcasted_iota(jnp.int32, (SEQ, LANES), 1)
    scale = ATT_HEAD_DIM ** -0.5 * float(np.log2(np.e))
    for hh in range(HEADS_PER_STEP):
        b0 = BIAS_LANE[hh]
        route = (lane_full >= b0) & (lane_full < b0 + nb)
        onehot = jnp.where(route & (lane_full - b0 == row_blk), 1.0, 0.0)
        qa_ref[hh] = jnp.where(head_lanes[hh], qn * scale, jnp.where(route, bias, 0.0)).astype(BF16)
        ka_ref[hh] = jnp.where(head_lanes[hh], kn, onehot).astype(BF16)
        vb_ref[hh] = jnp.where(head_lanes[hh], v, 1.0).astype(BF16)

    causal = _tril_mask(blk)
    for qb in range(nb):
        rows = slice(qb * blk, (qb + 1) * blk)
        accs = []
        for hh in range(HEADS_PER_STEP):
            qa = qa_ref[hh, rows, :]
            mx = None
            for kj in range(qb + 1):
                s = _dot_nt(qa, ka_ref[hh, kj * blk:(kj + 1) * blk, :])
                if kj == qb:
                    s = jnp.where(causal, s, NEG_INF)
                s_ref[hh, :, kj * blk:(kj + 1) * blk] = s
                t = jnp.maximum(s[:, :LANES], s[:, LANES:])
                mx = t if mx is None else jnp.maximum(mx, t)
            m = jnp.max(mx, axis=-1, keepdims=True)
            acc = None
            for kj in range(qb + 1):
                p = jnp.exp2(s_ref[hh, :, kj * blk:(kj + 1) * blk] - m).astype(BF16)
                pv = _dot(p, vb_ref[hh, kj * blk:(kj + 1) * blk, :])
                acc = pv if acc is None else acc + pv
            accs.append(acc)
        num = jnp.where(head_lanes[0], accs[0], accs[1])
        den = jnp.where(head_lanes[0], pltpu.roll(accs[0], ATT_HEAD_DIM, 1),
                        pltpu.roll(accs[1], ATT_HEAD_DIM, 1))
        o_ref[0, rows, :] = (num / den).astype(o_ref.dtype)


def _moba_attention(qkv, q_norm_w, k_norm_w, bsz):
    qkv3 = qkv.reshape(bsz, SEQ, 3 * ATT_WIDTH)
    n_pairs = ATT_WIDTH // LANES
    blk = lambda off: pl.BlockSpec((1, SEQ, LANES), lambda b, h: (b, 0, off + h))
    const = lambda b, h: (0, 0)
    tile_w = lambda w: jnp.tile(w, (1, HEADS_PER_STEP))
    out = pl.pallas_call(
        _moba_kernel,
        grid=(bsz, n_pairs),
        in_specs=[blk(0), blk(n_pairs), blk(2 * n_pairs),
                  pl.BlockSpec((1, LANES), const),
                  pl.BlockSpec((1, LANES), const)],
        out_specs=pl.BlockSpec((1, SEQ, LANES), lambda b, h: (b, 0, h)),
        out_shape=jax.ShapeDtypeStruct((bsz, SEQ, ATT_WIDTH), MIX_DTYPE),
        scratch_shapes=[pltpu.VMEM((HEADS_PER_STEP, SEQ, LANES), BF16),
                        pltpu.VMEM((HEADS_PER_STEP, SEQ, LANES), BF16),
                        pltpu.VMEM((HEADS_PER_STEP, SEQ, LANES), BF16),
                        pltpu.VMEM((HEADS_PER_STEP, MOBA_BLOCK, SEQ), F32)],
        compiler_params=pltpu.CompilerParams(dimension_semantics=("arbitrary", "arbitrary"),
                                             vmem_limit_bytes=VMEM_LIMIT),
        name="moba_attention",
    )(qkv3, qkv3, qkv3, tile_w(q_norm_w), tile_w(k_norm_w))
    return out.reshape(bsz * SEQ, ATT_WIDTH)


def _zero_conv_tail(pad_ref):
    pad_ref[0:SUBLANES, :] = jnp.zeros((SUBLANES, pad_ref.shape[1]), F32)


def _chunk_conv(x, w_ref, b_ref, pad_ref, width, rows, cols=slice(None)):
    pad_ref[SUBLANES:SUBLANES + rows, cols] = x
    acc = b_ref[:, cols] + w_ref[width - 1:width, cols] * x
    for j in range(width - 1):
        s = SUBLANES - (width - 1) + j
        acc = acc + w_ref[j:j + 1, cols] * pad_ref[s:s + rows, cols]
    pad_ref[0:SUBLANES, cols] = pad_ref[rows:rows + SUBLANES, cols]
    return acc


def _ssd_chunk(z_ref, xbc_ref, small_ref, cw_ref, cb_ref, dtb_ref, alog_ref, dx_ref, nw_ref,
               e64_ref, e128_ref, o_ref, pad_ref, st_ref, y_ref):
    xbc = _silu(_chunk_conv(xbc_ref[0], cw_ref, cb_ref, pad_ref, SSM_CONV, CHUNK))
    xs = xbc[:, :SSM_WIDTH]
    gn = SSM_GROUPS * SSM_STATE
    b_in = xbc[:, SSM_WIDTH:SSM_WIDTH + gn].astype(BF16)
    c_in = xbc[:, SSM_WIDTH + gn:].astype(BF16)

    lane = lax.broadcasted_iota(jnp.int32, (1, LANES), 1)
    head_lane = lane < SSM_HEADS
    dt = _softplus(small_ref[0] + dtb_ref[...])
    a = jnp.where(head_lane, -jnp.exp(alog_ref[...]), 0.0)
    tril = _tril_mask(CHUNK)
    tril_b = jnp.where(tril, 1.0, 0.0).astype(BF16)
    ld = _dot_exact_lhs(tril_b, dt * a)
    ld_t = ld.T
    dt_t = dt.T
    eld = jnp.exp(ld)
    w_state = dt * jnp.exp(ld[CHUNK - 1:CHUNK, :] - ld)
    e64 = e64_ref[...]
    eld_x = _dot_exact_rhs(eld, e64, EXPAND_TERMS)
    w_x = _dot_exact_rhs(w_state, e64, EXPAND_TERMS)
    ld_b = _dot_exact_rhs(ld, e128_ref[...], EXPAND_TERMS)

    lane2 = lax.broadcasted_iota(jnp.int32, (CHUNK, LANES), 1)
    hpg = SSM_HEADS // SSM_GROUPS
    for g in range(SSM_GROUPS):
        cb = _dot_nt(c_in[:, g * SSM_STATE:(g + 1) * SSM_STATE],
                     b_in[:, g * SSM_STATE:(g + 1) * SSM_STATE])
        for pair in range(hpg // 2):
            e0 = g * hpg + 2 * pair
            ms = []
            for e in (e0, e0 + 1):
                seg = ld_b[:, e * LANES:(e + 1) * LANES] - ld_t[e:e + 1, :]
                dec = jnp.exp(jnp.where(tril, seg, NEG_INF))
                ms.append(cb * dec * dt_t[e:e + 1, :])
            m_pair = jnp.concatenate(ms, axis=1).astype(BF16)
            xp = xs[:, e0 * SSM_HEAD_DIM:(e0 + 2) * SSM_HEAD_DIM]
            rhs = jnp.concatenate([jnp.where(lane2 < SSM_HEAD_DIM, xp, 0.0),
                                   jnp.where(lane2 >= SSM_HEAD_DIM, xp, 0.0)], axis=0)
            y_ref[:, e0 * SSM_HEAD_DIM:(e0 + 2) * SSM_HEAD_DIM] = _dot(m_pair, rhs.astype(BF16))

    for g in range(SSM_GROUPS):
        cols = slice(g * SSM_GROUP_WIDTH, (g + 1) * SSM_GROUP_WIDTH)
        st = st_ref[:, cols]
        c_g = c_in[:, g * SSM_STATE:(g + 1) * SSM_STATE]
        b_g = b_in[:, g * SSM_STATE:(g + 1) * SSM_STATE]
        y_ref[:, cols] += _dot(c_g, st.astype(BF16)) * eld_x[:, cols]
        chunk_state = _dot_tn(b_g, (xs[:, cols] * w_x[:, cols]).astype(BF16))
        st_ref[:, cols] = st * eld_x[CHUNK - 1:CHUNK, cols] + chunk_state

    y = (y_ref[...] + xs * dx_ref[...]) * _silu(z_ref[0])
    for g in range(SSM_GROUPS):
        cols = slice(g * SSM_GROUP_WIDTH, (g + 1) * SSM_GROUP_WIDTH)
        yg = y[:, cols]
        ms = jnp.mean(yg * yg, axis=-1, keepdims=True)
        o_ref[0, :, cols] = (yg * lax.rsqrt(ms + EPS) * nw_ref[:, cols]).astype(o_ref.dtype)


def _expand_matrix(rows, first_row, n_heads, rep):
    m = np.zeros((rows, n_heads * rep), np.float32)
    for e in range(n_heads):
        m[first_row + e, e * rep:(e + 1) * rep] = 1.0
    return jnp.asarray(m, dtype=BF16)


def _mlstm_chunk(qk_ref, v_ref, og_ref, small_ref, cw_ref, cb_ref, ib_ref, fb_ref, nw_ref,
                 ecf_ref, o_ref, pad_ref, st_ref, m_ref):
    dh = MLSTM_HEAD_DIM
    qk = _silu(_chunk_conv(qk_ref[0], cw_ref, cb_ref, pad_ref, MLSTM_CONV, CHUNK))
    small = small_ref[0]
    log_i = small + ib_ref[...]
    log_f = -_softplus(-(small + fb_ref[...]))
    tril = _tril_mask(CHUNK)
    tril_b = jnp.where(tril, 1.0, 0.0).astype(BF16)
    cum_f = _dot_exact_lhs(tril_b, log_f)
    cf_b = _dot_exact_rhs(cum_f, ecf_ref[...], EXPAND_TERMS)
    cf_t = cum_f.T
    li_t = log_i.T
    ones = jnp.ones((CHUNK, dh), BF16)

    for h in range(MLSTM_HEADS):
        cols = slice(h * dh, (h + 1) * dh)
        q = qk[:, h * dh:(h + 1) * dh].astype(BF16)
        k32 = qk[:, MLSTM_WIDTH + h * dh:MLSTM_WIDTH + (h + 1) * dh] * (dh ** -0.5)
        k = k32.astype(BF16)
        v_aug = jnp.concatenate([v_ref[0, :, cols].astype(BF16), ones], axis=1)
        cf_c = cf_b[:, cols]
        cf_r = cf_t[SMALL_F + h:SMALL_F + h + 1, :]
        li_r = li_t[SMALL_I + h:SMALL_I + h + 1, :]
        tot = cf_r[:, CHUNK - 1:CHUNK]
        st = st_ref[h]
        m_st = m_ref[h:h + 1, :]

        d = jnp.where(tril, cf_c - (cf_r - li_r), NEG_INF)
        inter = cf_c + m_st
        m_row = jnp.maximum(jnp.max(d, axis=-1, keepdims=True), inter)
        w_intra = jnp.exp(d - m_row)
        w_inter = jnp.exp(inter - m_row)
        scores = _dot_nt(q, k) * w_intra
        q_state = _dot(q, st.astype(BF16))
        num = _dot(scores.astype(BF16), v_aug[:, :dh]) + w_inter * q_state[:, :dh]
        den = jnp.sum(scores, axis=-1, keepdims=True) + w_inter * q_state[:, dh:]
        hv = num / jnp.maximum(jnp.abs(den), jnp.exp(-m_row))
        hv = hv * lax.rsqrt(jnp.mean(hv * hv, axis=-1, keepdims=True) + EPS) * nw_ref[:, cols]
        o_ref[0, :, cols] = (_sigmoid(og_ref[0, :, cols]) * hv).astype(o_ref.dtype)

        a_r = tot - cf_r + li_r
        m_loc = jnp.max(a_r, axis=-1, keepdims=True)
        w_r = jnp.exp(a_r - m_loc)
        loc = _dot((k32.T * w_r).astype(BF16), v_aug)
        m_new = jnp.maximum(tot + m_st, m_loc)
        s_old = jnp.exp(tot + m_st - m_new)
        s_new = jnp.exp(m_loc - m_new)
        st_ref[h] = st * jnp.concatenate([s_old, s_old], axis=1) + loc * jnp.concatenate([s_new, s_new], axis=1)
        m_ref[h:h + 1, :] = m_new


N_SSD_IN, N_SSD_SCRATCH = 11, 3
N_MLSTM_IN, N_MLSTM_SCRATCH = 10, 3


def _scan_kernel(*refs):
    ssd_in, refs = refs[:N_SSD_IN], refs[N_SSD_IN:]
    ml_in, refs = refs[:N_MLSTM_IN], refs[N_MLSTM_IN:]
    ssd_out, ml_out = refs[0], refs[1]
    ssd_scratch = refs[2:2 + N_SSD_SCRATCH]
    ml_scratch = refs[2 + N_SSD_SCRATCH:]

    @pl.when(pl.program_id(1) == 0)
    def _():
        _zero_conv_tail(ssd_scratch[0])
        _zero_conv_tail(ml_scratch[0])
        ssd_scratch[1][...] = jnp.zeros(ssd_scratch[1].shape, F32)
        ml_scratch[1][...] = jnp.zeros(ml_scratch[1].shape, F32)
        ml_scratch[2][...] = jnp.zeros(ml_scratch[2].shape, F32)

    _ssd_chunk(*ssd_in, ssd_out, *ssd_scratch)
    _mlstm_chunk(*ml_in, ml_out, *ml_scratch)


def _scan_mixers(z, xbc, small, m_qk, m_v, m_o, ssd_p, mlstm_p, bsz):
    s_conv_w, s_conv_b, dt_bias, a_log, d_skip, s_norm_w = ssd_p
    m_conv_w, m_conv_b, i_bias, f_bias, m_norm_w = mlstm_p
    nc = SEQ // CHUNK
    r3 = lambda a: a.reshape(bsz, SEQ, a.shape[-1])
    pad_at = lambda vec, at: jnp.pad(vec, (at, LANES - at - vec.shape[0]))[None, :]
    e64 = _expand_matrix(LANES, SMALL_DT, SSM_HEADS, SSM_HEAD_DIM)
    e128 = _expand_matrix(LANES, SMALL_DT, SSM_HEADS, LANES)
    ecf = _expand_matrix(LANES, SMALL_F, MLSTM_HEADS, LANES)
    chunk = lambda n: pl.BlockSpec((1, CHUNK, n), lambda b, c: (b, c, 0))
    const = lambda shape: pl.BlockSpec(shape, lambda b, c: (0, 0))
    small3 = r3(small)
    ssd_specs = [chunk(SSM_WIDTH), chunk(SSM_CONV_DIM), chunk(LANES),
                 const((SSM_CONV, SSM_CONV_DIM)), const((1, SSM_CONV_DIM)),
                 const((1, LANES)), const((1, LANES)),
                 const((1, SSM_WIDTH)), const((1, SSM_WIDTH)),
                 const((LANES, SSM_WIDTH)), const((LANES, SSM_HEADS * LANES))]
    ssd_args = [r3(z), r3(xbc), small3, s_conv_w, s_conv_b[None, :], pad_at(dt_bias, SMALL_DT),
                pad_at(a_log, SMALL_DT), jnp.repeat(d_skip, SSM_HEAD_DIM)[None, :], s_norm_w[None, :],
                e64, e128]
    ml_specs = [chunk(2 * MLSTM_WIDTH), chunk(MLSTM_WIDTH), chunk(MLSTM_WIDTH), chunk(LANES),
                const((MLSTM_CONV, 2 * MLSTM_WIDTH)), const((1, 2 * MLSTM_WIDTH)),
                const((1, LANES)), const((1, LANES)), const((1, MLSTM_WIDTH)),
                const((LANES, MLSTM_HEADS * LANES))]
    ml_args = [r3(m_qk), r3(m_v), r3(m_o), small3, m_conv_w, m_conv_b[None, :],
               pad_at(i_bias, SMALL_I), pad_at(f_bias, SMALL_F), m_norm_w[None, :], ecf]
    assert len(ssd_specs) == N_SSD_IN and len(ml_specs) == N_MLSTM_IN
    y_ssm, y_mlstm = pl.pallas_call(
        _scan_kernel,
        grid=(bsz, nc),
        in_specs=ssd_specs + ml_specs,
        out_specs=[chunk(SSM_WIDTH), chunk(MLSTM_WIDTH)],
        out_shape=[jax.ShapeDtypeStruct((bsz, SEQ, SSM_WIDTH), MIX_DTYPE),
                   jax.ShapeDtypeStruct((bsz, SEQ, MLSTM_WIDTH), MIX_DTYPE)],
        scratch_shapes=[pltpu.VMEM((SUBLANES + CHUNK, SSM_CONV_DIM), F32),
                        pltpu.VMEM((SSM_STATE, SSM_WIDTH), F32),
                        pltpu.VMEM((CHUNK, SSM_WIDTH), F32),
                        pltpu.VMEM((SUBLANES + CHUNK, 2 * MLSTM_WIDTH), F32),
                        pltpu.VMEM((MLSTM_HEADS, MLSTM_HEAD_DIM, 2 * MLSTM_HEAD_DIM), F32),
                        pltpu.VMEM((SUBLANES, LANES), F32)],
        compiler_params=pltpu.CompilerParams(dimension_semantics=("arbitrary", "arbitrary"),
                                             vmem_limit_bytes=VMEM_LIMIT),
        name="scan_mixers",
    )(*ssd_args, *ml_args)
    return y_ssm.reshape(bsz * SEQ, SSM_WIDTH), y_mlstm.reshape(bsz * SEQ, MLSTM_WIDTH)


FFN_TM = 512
FFN_SLAB = 1024


def _out_ffn_kernel(x_ref, ya_ref, ys_ref, ym_ref, wo_ref, nw_ref, wup_ref, cw_ref, cb_ref, wdn_ref,
                    o_ref, pad_ref):
    @pl.when(pl.program_id(1) == 0)
    def _():
        _zero_conv_tail(pad_ref)

    mix = _dot(ya_ref[...], wo_ref[0:ATT_WIDTH, :])
    mix += _dot(ys_ref[...], wo_ref[ATT_WIDTH:ATT_WIDTH + SSM_WIDTH, :])
    mix += _dot(ym_ref[...], wo_ref[ATT_WIDTH + SSM_WIDTH:, :])
    x1 = x_ref[...] + mix
    ms = jnp.mean(x1 * x1, axis=-1, keepdims=True)
    h = (x1 * lax.rsqrt(ms + EPS) * nw_ref[...]).astype(BF16)
    acc = x1
    for c0 in range(0, FFN_HIDDEN, FFN_SLAB):
        cols = slice(c0, min(c0 + FFN_SLAB, FFN_HIDDEN))
        vcols = slice(FFN_HIDDEN + cols.start, FFN_HIDDEN + cols.stop)
        gate = _chunk_conv(_dot(h, wup_ref[:, cols]), cw_ref, cb_ref, pad_ref, FFN_CONV, FFN_TM, cols)
        act = 0.5 * gate * (1.0 + lax.erf(gate * (2.0 ** -0.5)))
        act = act * _dot(h, wup_ref[:, vcols])
        acc = acc + _dot(act.astype(BF16), wdn_ref[cols, :])
    o_ref[...] = acc


def _out_ffn(x2d, y_att, y_ssm, y_mlstm, w_out, norm_w, w_up, conv_w, conv_b, w_down, bsz):
    t = x2d.shape[0]
    tiles = SEQ // FFN_TM
    row = lambda n: pl.BlockSpec((FFN_TM, n), lambda b, i: (b * tiles + i, 0))
    const = lambda shape: pl.BlockSpec(shape, lambda b, i: (0, 0), pipeline_mode=pl.Buffered(1))
    return pl.pallas_call(
        _out_ffn_kernel,
        grid=(bsz, tiles),
        in_specs=[row(D_MODEL), row(ATT_WIDTH), row(SSM_WIDTH), row(MLSTM_WIDTH),
                  const((D_MIX, D_MODEL)), const((1, D_MODEL)),
                  const((D_MODEL, 2 * FFN_HIDDEN)), const((FFN_CONV, FFN_HIDDEN)),
                  const((1, FFN_HIDDEN)), const((FFN_HIDDEN, D_MODEL))],
        out_specs=row(D_MODEL),
        out_shape=jax.ShapeDtypeStruct((t, D_MODEL), F32),
        scratch_shapes=[pltpu.VMEM((SUBLANES + FFN_TM, FFN_HIDDEN), F32)],
        compiler_params=pltpu.CompilerParams(dimension_semantics=("arbitrary", "arbitrary"),
                                             vmem_limit_bytes=VMEM_LIMIT),
        name="out_proj_convglu",
    )(x2d, y_att, y_ssm, y_mlstm, w_out, norm_w, w_up, conv_w, conv_b, w_down)


def _regroup_in_weights(w):
    offs = np.concatenate([[0], np.cumsum(np.array(PROJ_SIZES))])
    col = lambda i: w[:, int(offs[i]):int(offs[i + 1])]
    small = jnp.concatenate([col(5), col(9), col(10)], axis=1)
    small = jnp.pad(small, ((0, 0), (0, LANES - small.shape[1])))
    return jnp.concatenate([col(0), col(1), col(2), col(3), col(4), col(6), col(7), col(8), small],
                           axis=1).astype(BF16)


def kernel(x, norm1_w, w_in, q_norm_w, k_norm_w, ssm_conv_w, ssm_conv_b, ssm_dt_bias, ssm_a_log, ssm_d, ssm_norm_w, mlstm_conv_w, mlstm_conv_b, mlstm_i_bias, mlstm_f_bias, mlstm_norm_w, w_out, norm2_w, ffn_w_up, ffn_conv_w, ffn_conv_b, ffn_w_down):
    bsz, seq, d_model = x.shape
    assert seq == SEQ and d_model == D_MODEL
    depth = w_in.shape[0]
    x2d = x.reshape(bsz * seq, d_model)
    for l in range(depth):
        qkv, z, xbc, m_qk, m_v, m_o, small = _in_projection(
            x2d, norm1_w[l][None, :], _regroup_in_weights(w_in[l]))
        y_att = _moba_attention(qkv, q_norm_w[l][None, :], k_norm_w[l][None, :], bsz)
        y_ssm, y_mlstm = _scan_mixers(
            z, xbc, small, m_qk, m_v, m_o,
            (ssm_conv_w[l], ssm_conv_b[l], ssm_dt_bias[l], ssm_a_log[l], ssm_d[l], ssm_norm_w[l]),
            (mlstm_conv_w[l], mlstm_conv_b[l], mlstm_i_bias[l], mlstm_f_bias[l], mlstm_norm_w[l]), bsz)
        x2d = _out_ffn(x2d, y_att, y_ssm, y_mlstm, w_out[l].astype(BF16), norm2_w[l][None, :],
                       ffn_w_up[l].astype(BF16), ffn_conv_w[l], ffn_conv_b[l][None, :],
                       ffn_w_down[l].astype(BF16), bsz)
    return x2d.reshape(bsz, seq, d_model)
```

```python
import numpy as np
import jax
import jax.numpy as jnp
from jax import lax
from jax.experimental import pallas as pl
from jax.experimental.pallas import tpu as pltpu

F32 = jnp.float32
BF16 = jnp.bfloat16

D_MODEL = 1024
SEQ = 2048
D_MIX = 2 * D_MODEL
EPS = 1e-6
NEG_INF = -1e30

ATT_WIDTH = D_MIX // 4
ATT_HEAD_DIM = 64
ATT_HEADS = ATT_WIDTH // ATT_HEAD_DIM
MOBA_BLOCK = 256
MOBA_TOPK = 3
N_MOBA_BLOCKS = SEQ // MOBA_BLOCK

SSM_WIDTH = D_MIX // 2
SSM_HEAD_DIM = 64
SSM_HEADS = SSM_WIDTH // SSM_HEAD_DIM
SSM_GROUPS = 2
SSM_STATE = 128
SSM_CONV = 4
SSM_CHUNK = 128
SSM_CONV_DIM = SSM_WIDTH + 2 * SSM_GROUPS * SSM_STATE
SSM_GROUP_WIDTH = SSM_WIDTH // SSM_GROUPS

MLSTM_WIDTH = D_MIX // 4
MLSTM_HEAD_DIM = 128
MLSTM_HEADS = MLSTM_WIDTH // MLSTM_HEAD_DIM
MLSTM_CONV = 4
MLSTM_CHUNK = 128

FFN_HIDDEN = 11 * D_MODEL // 4
FFN_CONV = 3

PROJ_SIZES = (ATT_WIDTH, ATT_WIDTH, ATT_WIDTH,
              SSM_WIDTH, SSM_CONV_DIM, SSM_HEADS,
              2 * MLSTM_WIDTH, MLSTM_WIDTH, MLSTM_WIDTH,
              MLSTM_HEADS, MLSTM_HEADS)

LANES = 128
SUBLANES = 8
CHUNK = 128
SMALL_DT = 0
SMALL_I = SSM_HEADS
SMALL_F = SSM_HEADS + MLSTM_HEADS
VMEM_LIMIT = 56 * 1024 * 1024
EXPAND_TERMS = 2
MIX_DTYPE = BF16


def _dot(a, b):
    return jnp.dot(a, b, preferred_element_type=F32)


def _dot_nt(a, b):
    return lax.dot_general(a, b, (((1,), (1,)), ((), ())), preferred_element_type=F32)


def _dot_tn(a, b):
    return lax.dot_general(a, b, (((0,), (0,)), ((), ())), preferred_element_type=F32)


def _split(v, terms):
    out = []
    for _ in range(terms - 1):
        part = v.astype(BF16)
        out.append(part)
        v = v - part.astype(F32)
    out.append(v.astype(BF16))
    return out


def _dot_exact_lhs(e, v, terms=3):
    parts = _split(v, terms)
    acc = _dot(e, parts[0])
    for part in parts[1:]:
        acc = acc + _dot(e, part)
    return acc


def _dot_exact_rhs(v, e, terms=3):
    parts = _split(v, terms)
    acc = _dot(parts[0], e)
    for part in parts[1:]:
        acc = acc + _dot(part, e)
    return acc


def _sigmoid(x):
    return 1.0 / (1.0 + jnp.exp(-x))


def _silu(x):
    return x * _sigmoid(x)


def _softplus(x):
    return jnp.maximum(x, 0.0) + jnp.log1p(jnp.exp(-jnp.abs(x)))


def _tril_mask(n):
    r = lax.broadcasted_iota(jnp.int32, (n, n), 0)
    c = lax.broadcasted_iota(jnp.int32, (n, n), 1)
    return c <= r


IN_TM = 512
IN_GROUPS = (3 * ATT_WIDTH, SSM_WIDTH, SSM_CONV_DIM, 2 * MLSTM_WIDTH, MLSTM_WIDTH, MLSTM_WIDTH, LANES)
IN_COLS = sum(IN_GROUPS)
DOT_N = 512
CONV_DOT_N = 256


IN_XBC, IN_MQK = 2, 3


def _inproj_kernel(x_ref, nw_ref, w_ref, scw_ref, scb_ref, mcw_ref, mcb_ref, *refs):
    out_refs = refs[:len(IN_GROUPS)]
    spad_ref, mpad_ref = refs[len(IN_GROUPS):]

    @pl.when(pl.program_id(0) % (SEQ // IN_TM) == 0)
    def _():
        _zero_conv_tail(spad_ref)
        _zero_conv_tail(mpad_ref)

    conv = {IN_XBC: (scw_ref, scb_ref, spad_ref, SSM_CONV),
            IN_MQK: (mcw_ref, mcb_ref, mpad_ref, MLSTM_CONV)}
    x = x_ref[...]
    ms = jnp.mean(x * x, axis=-1, keepdims=True)
    h = (x * lax.rsqrt(ms + EPS) * nw_ref[...]).astype(BF16)
    plain, with_conv, off = [], [], 0
    for gi, ref in enumerate(out_refs):
        n = ref.shape[-1]
        step = CONV_DOT_N if gi in conv else DOT_N
        for c in range(0, n, step):
            (with_conv if gi in conv else plain).append((gi, slice(c, min(c + step, n)), off))
        off += n
    slabs = []
    while plain or with_conv:
        if with_conv:
            slabs.append(with_conv.pop(0))
        if plain:
            slabs.append(plain.pop(0))

    def epilogue(y, gi, cols):
        if gi in conv:
            cw_ref, cb_ref, pad_ref, width = conv[gi]
            y = _silu(_chunk_conv(y, cw_ref, cb_ref, pad_ref, width, IN_TM, cols))
        out_refs[gi][:, cols] = y

    pending = None
    for gi, cols, off in slabs:
        y = _dot(h, w_ref[:, off + cols.start:off + cols.stop])
        if pending is not None:
            epilogue(*pending)
        pending = (y, gi, cols)
    epilogue(*pending)


def _in_projection(x2d, norm_w, w_cat, ssm_conv, mlstm_conv):
    t = x2d.shape[0]
    const = lambda i: (0, 0)
    row = lambda i: (i, 0)
    conv_specs = [pl.BlockSpec(a.shape, const) for a in (*ssm_conv, *mlstm_conv)]
    return pl.pallas_call(
        _inproj_kernel,
        grid=(t // IN_TM,),
        in_specs=[pl.BlockSpec((IN_TM, D_MODEL), row),
                  pl.BlockSpec((1, D_MODEL), const),
                  pl.BlockSpec((D_MODEL, IN_COLS), const, pipeline_mode=pl.Buffered(1))] + conv_specs,
        out_specs=[pl.BlockSpec((IN_TM, n), row) for n in IN_GROUPS],
        out_shape=[jax.ShapeDtypeStruct((t, n), F32) for n in IN_GROUPS],
        scratch_shapes=[pltpu.VMEM((SUBLANES + IN_TM, IN_GROUPS[IN_XBC]), F32),
                        pltpu.VMEM((SUBLANES + IN_TM, IN_GROUPS[IN_MQK]), F32)],
        compiler_params=pltpu.CompilerParams(dimension_semantics=("arbitrary",),
                                             vmem_limit_bytes=VMEM_LIMIT),
        name="in_projection",
    )(x2d, norm_w, w_cat, *ssm_conv, *mlstm_conv)


HEADS_PER_STEP = LANES // ATT_HEAD_DIM
BIAS_LANE = (ATT_HEAD_DIM, 0)


def _moba_kernel(q_ref, k_ref, v_ref, qw_ref, kw_ref, oh_ref, o_ref, qa_ref, ka_ref, vb_ref, s_ref):
    nb = N_MOBA_BLOCKS
    blk = MOBA_BLOCK
    lane = lax.broadcasted_iota(jnp.int32, (1, LANES), 1)
    head_lanes = [(lane >= hh * ATT_HEAD_DIM) & (lane < (hh + 1) * ATT_HEAD_DIM)
                  for hh in range(HEADS_PER_STEP)]
    same_head = jnp.where(lax.broadcasted_iota(jnp.int32, (LANES, LANES), 0) // ATT_HEAD_DIM
                          == lax.broadcasted_iota(jnp.int32, (LANES, LANES), 1) // ATT_HEAD_DIM,
                          1.0, 0.0).astype(BF16)

    def head_rmsnorm(x, w):
        ss = _dot_exact_rhs(x * x, same_head, terms=2)
        return x * lax.rsqrt(ss * (1.0 / ATT_HEAD_DIM) + EPS) * w

    qn = head_rmsnorm(q_ref[0], qw_ref[...])
    kn = head_rmsnorm(k_ref[0], kw_ref[...])
    v = v_ref[0]
    k_mean = jnp.sum(kn.reshape(nb, blk, LANES), axis=1) * (1.0 / blk)
    yield

    q_hi, q_lo = _split(qn, 2)
    k_hi, k_lo = [], []
    for hh in range(HEADS_PER_STEP):
        km = jnp.where(head_lanes[hh], k_mean, 0.0)
        hi = km.astype(BF16).astype(F32)
        k_hi.append(hi)
        k_lo.append(km - hi)
    g_hi = _dot_nt(jnp.concatenate([k_hi[0], k_lo[0], k_hi[1], k_lo[1]], axis=0).astype(BF16), q_hi)
    g_lo = _dot_nt(jnp.concatenate(k_hi, axis=0).astype(BF16), q_lo)

    cand = lax.broadcasted_iota(jnp.int32, (nb, SEQ), 0)
    own = lax.broadcasted_iota(jnp.int32, (nb, SEQ), 1) // blk
    valid = cand < own
    yield
    bias_rows = {}
    for hh in range(HEADS_PER_STEP):
        gate = (g_hi[2 * hh * nb:(2 * hh + 1) * nb] + g_hi[(2 * hh + 1) * nb:(2 * hh + 2) * nb]
                + g_lo[hh * nb:(hh + 1) * nb])
        gm = jnp.where(valid, gate, NEG_INF)
        rank = jnp.zeros((nb, SEQ), jnp.int32)
        for j in range(nb):
            gj = gm[j:j + 1, :]
            ahead = (gj > gm) | ((gj == gm) & (j < cand))
            rank = rank + ahead.astype(jnp.int32)
        allowed = (valid & (rank < MOBA_TOPK)) | (cand == own)
        bias_rows[BIAS_LANE[hh]] = jnp.where(allowed, 0.0, NEG_INF)
    zeros = jnp.zeros((ATT_HEAD_DIM - nb, SEQ), F32)
    bias_t = jnp.concatenate([bias_rows[0], zeros, bias_rows[ATT_HEAD_DIM], zeros], axis=0)
    bias = bias_t.T
    yield

    scale = ATT_HEAD_DIM ** -0.5 * float(np.log2(np.e))
    q_b = (qn * scale).astype(BF16)
    k_b = kn.astype(BF16)
    v_b = v.astype(BF16)
    bias_b = bias.astype(BF16)
    onehot_b = oh_ref[...]
    ones_b = jnp.ones((SEQ, LANES), BF16)
    for hh in range(HEADS_PER_STEP):
        qa_ref[hh] = jnp.where(head_lanes[hh], q_b, bias_b)
        ka_ref[hh] = jnp.where(head_lanes[hh], k_b, onehot_b)
        vb_ref[hh] = jnp.where(head_lanes[hh], v_b, ones_b)

    yield
    causal = _tril_mask(blk)
    for qb in range(nb):
        rows = slice(qb * blk, (qb + 1) * blk)
        accs = []
        for hh in range(HEADS_PER_STEP):
            qa = qa_ref[hh, rows, :]
            mx = None
            for kj in range(qb + 1):
                s = _dot_nt(qa, ka_ref[hh, kj * blk:(kj + 1) * blk, :])
                if kj == qb:
                    s = jnp.where(causal, s, NEG_INF)
                s_ref[hh, :, kj * blk:(kj + 1) * blk] = s
                t = jnp.maximum(s[:, :LANES], s[:, LANES:])
                mx = t if mx is None else jnp.maximum(mx, t)
                yield
            m = jnp.max(mx, axis=-1, keepdims=True)
            acc = None
            for kj in range(qb + 1):
                p = jnp.exp2(s_ref[hh, :, kj * blk:(kj + 1) * blk] - m).astype(BF16)
                pv = _dot(p, vb_ref[hh, kj * blk:(kj + 1) * blk, :])
                acc = pv if acc is None else acc + pv
                yield
            accs.append(acc)
        num = jnp.where(head_lanes[0], accs[0], accs[1])
        den = jnp.where(head_lanes[0], pltpu.roll(accs[0], ATT_HEAD_DIM, 1),
                        pltpu.roll(accs[1], ATT_HEAD_DIM, 1))
        o_ref[0, rows, :] = (num / den).astype(o_ref.dtype)


N_MOBA_IN, N_MOBA_SCRATCH = 6, 4
N_HEAD_PAIRS = ATT_WIDTH // LANES


def _block_onehot():
    m = np.zeros((SEQ, LANES), np.float32)
    t = np.arange(SEQ)
    for b0 in BIAS_LANE:
        m[t, b0 + t // MOBA_BLOCK] = 1.0
    return jnp.asarray(m, dtype=BF16)


def _zero_conv_tail(pad_ref):
    pad_ref[0:SUBLANES, :] = jnp.zeros((SUBLANES, pad_ref.shape[1]), F32)


def _chunk_conv(x, w_ref, b_ref, pad_ref, width, rows, cols=slice(None)):
    pad_ref[SUBLANES:SUBLANES + rows, cols] = x
    acc = b_ref[:, cols] + w_ref[width - 1:width, cols] * x
    for j in range(width - 1):
        s = SUBLANES - (width - 1) + j
        acc = acc + w_ref[j:j + 1, cols] * pad_ref[s:s + rows, cols]
    pad_ref[0:SUBLANES, cols] = pad_ref[rows:rows + SUBLANES, cols]
    return acc


def _ssd_chunk(z_ref, xbc_ref, small_ref, dtb_ref, alog_ref, dx_ref, nw_ref,
               e64_ref, e128_ref, o_ref, st_ref, y_ref):
    xs = xbc_ref[:, :SSM_WIDTH]
    gn = SSM_GROUPS * SSM_STATE
    b_in = xbc_ref[:, SSM_WIDTH:SSM_WIDTH + gn].astype(BF16)
    c_in = xbc_ref[:, SSM_WIDTH + gn:].astype(BF16)

    lane = lax.broadcasted_iota(jnp.int32, (1, LANES), 1)
    head_lane = lane < SSM_HEADS
    dt = _softplus(small_ref[...] + dtb_ref[...])
    a = jnp.where(head_lane, -jnp.exp(alog_ref[...]), 0.0)
    tril = _tril_mask(CHUNK)
    tril_b = jnp.where(tril, 1.0, 0.0).astype(BF16)
    ld = _dot_exact_lhs(tril_b, dt * a)
    ld_t = ld.T
    dt_t = dt.T
    eld = jnp.exp(ld)
    w_state = dt * jnp.exp(ld[CHUNK - 1:CHUNK, :] - ld)
    e64 = e64_ref[...]
    eld_x = _dot_exact_rhs(eld, e64, EXPAND_TERMS)
    w_x = _dot_exact_rhs(w_state, e64, EXPAND_TERMS)
    ld_b = _dot_exact_rhs(ld, e128_ref[...], EXPAND_TERMS)
    yield

    lane2 = lax.broadcasted_iota(jnp.int32, (CHUNK, LANES), 1)
    hpg = SSM_HEADS // SSM_GROUPS
    for g in range(SSM_GROUPS):
        cb = _dot_nt(c_in[:, g * SSM_STATE:(g + 1) * SSM_STATE],
                     b_in[:, g * SSM_STATE:(g + 1) * SSM_STATE])
        for pair in range(hpg // 2):
            e0 = g * hpg + 2 * pair
            ms = []
            for e in (e0, e0 + 1):
                seg = ld_b[:, e * LANES:(e + 1) * LANES] - ld_t[e:e + 1, :]
                dec = jnp.exp(jnp.where(tril, seg, NEG_INF))
                ms.append(cb * dec * dt_t[e:e + 1, :])
            m_pair = jnp.concatenate(ms, axis=1).astype(BF16)
            xp = xs[:, e0 * SSM_HEAD_DIM:(e0 + 2) * SSM_HEAD_DIM]
            rhs = jnp.concatenate([jnp.where(lane2 < SSM_HEAD_DIM, xp, 0.0),
                                   jnp.where(lane2 >= SSM_HEAD_DIM, xp, 0.0)], axis=0)
            y_ref[:, e0 * SSM_HEAD_DIM:(e0 + 2) * SSM_HEAD_DIM] = _dot(m_pair, rhs.astype(BF16))
            yield

    for g in range(SSM_GROUPS):
        cols = slice(g * SSM_GROUP_WIDTH, (g + 1) * SSM_GROUP_WIDTH)
        st = st_ref[:, cols]
        c_g = c_in[:, g * SSM_STATE:(g + 1) * SSM_STATE]
        b_g = b_in[:, g * SSM_STATE:(g + 1) * SSM_STATE]
        y_ref[:, cols] += _dot(c_g, st.astype(BF16)) * eld_x[:, cols]
        chunk_state = _dot_tn(b_g, (xs[:, cols] * w_x[:, cols]).astype(BF16))
        st_ref[:, cols] = st * eld_x[CHUNK - 1:CHUNK, cols] + chunk_state
        yield

    y = (y_ref[...] + xs * dx_ref[...]) * _silu(z_ref[...])
    for g in range(SSM_GROUPS):
        cols = slice(g * SSM_GROUP_WIDTH, (g + 1) * SSM_GROUP_WIDTH)
        yg = y[:, cols]
        ms = jnp.mean(yg * yg, axis=-1, keepdims=True)
        o_ref[:, cols] = (yg * lax.rsqrt(ms + EPS) * nw_ref[:, cols]).astype(o_ref.dtype)
        yield


def _expand_matrix(rows, first_row, n_heads, rep):
    m = np.zeros((rows, n_heads * rep), np.float32)
    for e in range(n_heads):
        m[first_row + e, e * rep:(e + 1) * rep] = 1.0
    return jnp.asarray(m, dtype=BF16)


def _mlstm_chunk(qk_ref, v_ref, og_ref, small_ref, ib_ref, fb_ref, nw_ref,
                 ecf_ref, o_ref, st_ref, m_ref):
    dh = MLSTM_HEAD_DIM
    qk = qk_ref[...]
    small = small_ref[...]
    log_i = small + ib_ref[...]
    log_f = -_softplus(-(small + fb_ref[...]))
    tril = _tril_mask(CHUNK)
    tril_b = jnp.where(tril, 1.0, 0.0).astype(BF16)
    cum_f = _dot_exact_lhs(tril_b, log_f)
    cf_b = _dot_exact_rhs(cum_f, ecf_ref[...], EXPAND_TERMS)
    cf_t = cum_f.T
    li_t = log_i.T
    ones = jnp.ones((CHUNK, dh), BF16)
    yield

    for h in range(MLSTM_HEADS):
        cols = slice(h * dh, (h + 1) * dh)
        q = qk[:, h * dh:(h + 1) * dh].astype(BF16)
        k32 = qk[:, MLSTM_WIDTH + h * dh:MLSTM_WIDTH + (h + 1) * dh] * (dh ** -0.5)
        k = k32.astype(BF16)
        v_aug = jnp.concatenate([v_ref[:, cols].astype(BF16), ones], axis=1)
        cf_c = cf_b[:, cols]
        cf_r = cf_t[SMALL_F + h:SMALL_F + h + 1, :]
        li_r = li_t[SMALL_I + h:SMALL_I + h + 1, :]
        tot = cf_r[:, CHUNK - 1:CHUNK]
        st = st_ref[h]
        m_st = m_ref[h:h + 1, :]

        d = jnp.where(tril, cf_c - (cf_r - li_r), NEG_INF)
        inter = cf_c + m_st
        m_row = jnp.maximum(jnp.max(d, axis=-1, keepdims=True), inter)
        w_intra = jnp.exp(d - m_row)
        w_inter = jnp.exp(inter - m_row)
        scores = _dot_nt(q, k) * w_intra
        q_state = _dot(q, st.astype(BF16))
        num = _dot(scores.astype(BF16), v_aug[:, :dh]) + w_inter * q_state[:, :dh]
        den = jnp.sum(scores, axis=-1, keepdims=True) + w_inter * q_state[:, dh:]
        hv = num / jnp.maximum(jnp.abs(den), jnp.exp(-m_row))
        hv = hv * lax.rsqrt(jnp.mean(hv * hv, axis=-1, keepdims=True) + EPS) * nw_ref[:, cols]
        o_ref[:, cols] = (_sigmoid(og_ref[:, cols]) * hv).astype(o_ref.dtype)

        a_r = tot - cf_r + li_r
        m_loc = jnp.max(a_r, axis=-1, keepdims=True)
        w_r = jnp.exp(a_r - m_loc)
        loc = _dot((k32.T * w_r).astype(BF16), v_aug)
        m_new = jnp.maximum(tot + m_st, m_loc)
        s_old = jnp.exp(tot + m_st - m_new)
        s_new = jnp.exp(m_loc - m_new)
        st_ref[h] = st * jnp.concatenate([s_old, s_old], axis=1) + loc * jnp.concatenate([s_new, s_new], axis=1)
        m_ref[h:h + 1, :] = m_new
        yield


N_SSD_IN, N_SSD_DATA, N_SSD_SCRATCH = 9, 3, 2
N_MLSTM_IN, N_MLSTM_DATA, N_MLSTM_SCRATCH = 8, 4, 2
SCAN_CHUNKS = 4
SCAN_ROWS = SCAN_CHUNKS * CHUNK


def _scan_init(ssd_scratch, ml_scratch):
    ssd_scratch[0][...] = jnp.zeros(ssd_scratch[0].shape, F32)
    ml_scratch[0][...] = jnp.zeros(ml_scratch[0].shape, F32)
    ml_scratch[1][...] = jnp.zeros(ml_scratch[1].shape, F32)


def _scan_step(ssd_in, ml_in, ssd_out, ml_out, ssd_scratch, ml_scratch):
    st_ref, y_ref = ssd_scratch
    for ci in range(SCAN_CHUNKS):
        rows = lambda r: r.at[0, pl.ds(ci * CHUNK, CHUNK)]
        yield from _ssd_chunk(*[rows(r) for r in ssd_in[:N_SSD_DATA]], *ssd_in[N_SSD_DATA:],
                              rows(ssd_out), st_ref, y_ref.at[ci])
        yield from _mlstm_chunk(*[rows(r) for r in ml_in[:N_MLSTM_DATA]], *ml_in[N_MLSTM_DATA:],
                                rows(ml_out), *ml_scratch)


def _interleave(streams):
    done = [0] * len(streams)
    alive = list(range(len(streams)))
    while alive:
        i = min(alive, key=lambda j: done[j] / streams[j][1])
        try:
            next(streams[i][0])
            done[i] += 1
        except StopIteration:
            alive.remove(i)


def _mixers_kernel(*refs):
    moba_in, refs = refs[:N_MOBA_IN], refs[N_MOBA_IN:]
    ssd_in, refs = refs[:N_SSD_IN], refs[N_SSD_IN:]
    ml_in, refs = refs[:N_MLSTM_IN], refs[N_MLSTM_IN:]
    (att_out, ssd_out, ml_out), refs = refs[:3], refs[3:]
    moba_scratch, refs = refs[:N_MOBA_SCRATCH], refs[N_MOBA_SCRATCH:]
    ssd_scratch, ml_scratch = refs[:N_SSD_SCRATCH], refs[N_SSD_SCRATCH:]
    @pl.when(pl.program_id(1) == 0)
    def _():
        _scan_init(ssd_scratch, ml_scratch)

    scan = _scan_step(ssd_in, ml_in, ssd_out, ml_out, ssd_scratch, ml_scratch)
    moba = _moba_kernel(*moba_in, att_out, *moba_scratch)
    moba_pieces = 4 + 2 * HEADS_PER_STEP * (N_MOBA_BLOCKS * (N_MOBA_BLOCKS + 1) // 2)
    scan_pieces = SCAN_CHUNKS * ((1 + SSM_HEADS // 2 + 2 * SSM_GROUPS) + (1 + MLSTM_HEADS))
    _interleave([(moba, moba_pieces), (scan, scan_pieces)])


def _mixers(qkv, z, xbc, small, m_qk, m_v, m_o, q_norm_w, k_norm_w, ssd_p, mlstm_p, bsz):
    dt_bias, a_log, d_skip, s_norm_w = ssd_p
    i_bias, f_bias, m_norm_w = mlstm_p
    assert N_HEAD_PAIRS == SEQ // SCAN_ROWS
    r3 = lambda a: a.reshape(bsz, SEQ, a.shape[-1])
    qkv3 = r3(qkv)
    head_blk = lambda off: pl.BlockSpec((1, SEQ, LANES), lambda b, c: (b, 0, off + c))
    tile_w = lambda w: jnp.tile(w, (1, HEADS_PER_STEP))
    moba_specs = [head_blk(0), head_blk(N_HEAD_PAIRS), head_blk(2 * N_HEAD_PAIRS),
                  pl.BlockSpec((1, LANES), lambda b, c: (0, 0)),
                  pl.BlockSpec((1, LANES), lambda b, c: (0, 0)),
                  pl.BlockSpec((SEQ, LANES), lambda b, c: (0, 0))]
    moba_args = [qkv3, qkv3, qkv3, tile_w(q_norm_w), tile_w(k_norm_w), _block_onehot()]
    pad_at = lambda vec, at: jnp.pad(vec, (at, LANES - at - vec.shape[0]))[None, :]
    e64 = _expand_matrix(LANES, SMALL_DT, SSM_HEADS, SSM_HEAD_DIM)
    e128 = _expand_matrix(LANES, SMALL_DT, SSM_HEADS, LANES)
    ecf = _expand_matrix(LANES, SMALL_F, MLSTM_HEADS, LANES)
    chunk = lambda n: pl.BlockSpec((1, SCAN_ROWS, n), lambda b, c: (b, c, 0))
    const = lambda shape: pl.BlockSpec(shape, lambda b, c: (0, 0))
    small3 = r3(small)
    ssd_specs = [chunk(SSM_WIDTH), chunk(SSM_CONV_DIM), chunk(LANES),
                 const((1, LANES)), const((1, LANES)),
                 const((1, SSM_WIDTH)), const((1, SSM_WIDTH)),
                 const((LANES, SSM_WIDTH)), const((LANES, SSM_HEADS * LANES))]
    ssd_args = [r3(z), r3(xbc), small3, pad_at(dt_bias, SMALL_DT),
                pad_at(a_log, SMALL_DT), jnp.repeat(d_skip, SSM_HEAD_DIM)[None, :], s_norm_w[None, :],
                e64, e128]
    ml_specs = [chunk(2 * MLSTM_WIDTH), chunk(MLSTM_WIDTH), chunk(MLSTM_WIDTH), chunk(LANES),
                const((1, LANES)), const((1, LANES)), const((1, MLSTM_WIDTH)),
                const((LANES, MLSTM_HEADS * LANES))]
    ml_args = [r3(m_qk), r3(m_v), r3(m_o), small3,
               pad_at(i_bias, SMALL_I), pad_at(f_bias, SMALL_F), m_norm_w[None, :], ecf]
    assert len(ssd_specs) == N_SSD_IN and len(ml_specs) == N_MLSTM_IN and len(moba_specs) == N_MOBA_IN
    y_att, y_ssm, y_mlstm = pl.pallas_call(
        _mixers_kernel,
        grid=(bsz, N_HEAD_PAIRS),
        in_specs=moba_specs + ssd_specs + ml_specs,
        out_specs=[pl.BlockSpec((1, SEQ, LANES), lambda b, c: (b, 0, c)),
                   chunk(SSM_WIDTH), chunk(MLSTM_WIDTH)],
        out_shape=[jax.ShapeDtypeStruct((bsz, SEQ, ATT_WIDTH), MIX_DTYPE),
                   jax.ShapeDtypeStruct((bsz, SEQ, SSM_WIDTH), MIX_DTYPE),
                   jax.ShapeDtypeStruct((bsz, SEQ, MLSTM_WIDTH), MIX_DTYPE)],
        scratch_shapes=[pltpu.VMEM((HEADS_PER_STEP, SEQ, LANES), BF16),
                        pltpu.VMEM((HEADS_PER_STEP, SEQ, LANES), BF16),
                        pltpu.VMEM((HEADS_PER_STEP, SEQ, LANES), BF16),
                        pltpu.VMEM((HEADS_PER_STEP, MOBA_BLOCK, SEQ), F32),
                        pltpu.VMEM((SSM_STATE, SSM_WIDTH), F32),
                        pltpu.VMEM((SCAN_CHUNKS, CHUNK, SSM_WIDTH), F32),
                        pltpu.VMEM((MLSTM_HEADS, MLSTM_HEAD_DIM, 2 * MLSTM_HEAD_DIM), F32),
                        pltpu.VMEM((SUBLANES, LANES), F32)],
        compiler_params=pltpu.CompilerParams(dimension_semantics=("arbitrary", "arbitrary"),
                                             vmem_limit_bytes=VMEM_LIMIT),
        name="mixers",
    )(*moba_args, *ssd_args, *ml_args)
    t = bsz * SEQ
    return y_att.reshape(t, ATT_WIDTH), y_ssm.reshape(t, SSM_WIDTH), y_mlstm.reshape(t, MLSTM_WIDTH)


FFN_TM = 512
FFN_SLAB = 1024


def _out_ffn_kernel(x_ref, ya_ref, ys_ref, ym_ref, wo_ref, nw_ref, wup_ref, cw_ref, cb_ref, wdn_ref,
                    o_ref, pad_ref):
    @pl.when(pl.program_id(1) == 0)
    def _():
        _zero_conv_tail(pad_ref)

    mix = _dot(ya_ref[...], wo_ref[0:ATT_WIDTH, :])
    mix += _dot(ys_ref[...], wo_ref[ATT_WIDTH:ATT_WIDTH + SSM_WIDTH, :])
    mix += _dot(ym_ref[...], wo_ref[ATT_WIDTH + SSM_WIDTH:, :])
    x1 = x_ref[...] + mix
    ms = jnp.mean(x1 * x1, axis=-1, keepdims=True)
    h = (x1 * lax.rsqrt(ms + EPS) * nw_ref[...]).astype(BF16)
    acc = x1
    for c0 in range(0, FFN_HIDDEN, FFN_SLAB):
        cols = slice(c0, min(c0 + FFN_SLAB, FFN_HIDDEN))
        vcols = slice(FFN_HIDDEN + cols.start, FFN_HIDDEN + cols.stop)
        gate = _chunk_conv(_dot(h, wup_ref[:, cols]), cw_ref, cb_ref, pad_ref, FFN_CONV, FFN_TM, cols)
        act = 0.5 * gate * (1.0 + lax.erf(gate * (2.0 ** -0.5)))
        act = act * _dot(h, wup_ref[:, vcols])
        acc = acc + _dot(act.astype(BF16), wdn_ref[cols, :])
    o_ref[...] = acc


def _out_ffn(x2d, y_att, y_ssm, y_mlstm, w_out, norm_w, w_up, conv_w, conv_b, w_down, bsz):
    t = x2d.shape[0]
    tiles = SEQ // FFN_TM
    row = lambda n: pl.BlockSpec((FFN_TM, n), lambda b, i: (b * tiles + i, 0))
    const = lambda shape: pl.BlockSpec(shape, lambda b, i: (0, 0), pipeline_mode=pl.Buffered(1))
    return pl.pallas_call(
        _out_ffn_kernel,
        grid=(bsz, tiles),
        in_specs=[row(D_MODEL), row(ATT_WIDTH), row(SSM_WIDTH), row(MLSTM_WIDTH),
                  const((D_MIX, D_MODEL)), const((1, D_MODEL)),
                  const((D_MODEL, 2 * FFN_HIDDEN)), const((FFN_CONV, FFN_HIDDEN)),
                  const((1, FFN_HIDDEN)), const((FFN_HIDDEN, D_MODEL))],
        out_specs=row(D_MODEL),
        out_shape=jax.ShapeDtypeStruct((t, D_MODEL), F32),
        scratch_shapes=[pltpu.VMEM((SUBLANES + FFN_TM, FFN_HIDDEN), F32)],
        compiler_params=pltpu.CompilerParams(dimension_semantics=("arbitrary", "arbitrary"),
                                             vmem_limit_bytes=VMEM_LIMIT),
        name="out_proj_convglu",
    )(x2d, y_att, y_ssm, y_mlstm, w_out, norm_w, w_up, conv_w, conv_b, w_down)


def _regroup_in_weights(w):
    offs = np.concatenate([[0], np.cumsum(np.array(PROJ_SIZES))])
    col = lambda i: w[:, int(offs[i]):int(offs[i + 1])]
    small = jnp.concatenate([col(5), col(9), col(10)], axis=1)
    small = jnp.pad(small, ((0, 0), (0, LANES - small.shape[1])))
    return jnp.concatenate([col(0), col(1), col(2), col(3), col(4), col(6), col(7), col(8), small],
                           axis=1).astype(BF16)


def kernel(x, norm1_w, w_in, q_norm_w, k_norm_w, ssm_conv_w, ssm_conv_b, ssm_dt_bias, ssm_a_log, ssm_d, ssm_norm_w, mlstm_conv_w, mlstm_conv_b, mlstm_i_bias, mlstm_f_bias, mlstm_norm_w, w_out, norm2_w, ffn_w_up, ffn_conv_w, ffn_conv_b, ffn_w_down):
    bsz, seq, d_model = x.shape
    assert seq == SEQ and d_model == D_MODEL
    depth = w_in.shape[0]
    x2d = x.reshape(bsz * seq, d_model)
    for l in range(depth):
        qkv, z, xbc, m_qk, m_v, m_o, small = _in_projection(
            x2d, norm1_w[l][None, :], _regroup_in_weights(w_in[l]),
            (ssm_conv_w[l], ssm_conv_b[l][None, :]), (mlstm_conv_w[l], mlstm_conv_b[l][None, :]))
        y_att, y_ssm, y_mlstm = _mixers(
            qkv, z, xbc, small, m_qk, m_v, m_o, q_norm_w[l][None, :], k_norm_w[l][None, :],
            (ssm_dt_bias[l], ssm_a_log[l], ssm_d[l], ssm_norm_w[l]),
            (mlstm_i_bias[l], mlstm_f_bias[l], mlstm_norm_w[l]), bsz)
        x2d = _out_ffn(x2d, y_att, y_ssm, y_mlstm, w_out[l].astype(BF16), norm2_w[l][None, :],
                       ffn_w_up[l].astype(BF16), ffn_conv_w[l], ffn_conv_b[l][None, :],
                       ffn_w_down[l].astype(BF16), bsz)
    return x2d.reshape(bsz, seq, d_model)
```
